```python
import math
import jax, jax.numpy as jnp
from jax import lax
import numpy as np

D_MODEL = 1024
BATCH = 8
SEQ = 2048
DEPTH = 2

NORM_EPS = 1e-6
N_BRANCH = 3
N_MOD = 6

RW_HEADS = 8
RW_HEAD_DIM = 64
RW_WIDTH = RW_HEADS * RW_HEAD_DIM
RW_DECAY_LORA = 64
RW_AAA_LORA = 64
RW_GATE_LORA = 128
RW_GN_EPS = 64e-5
RW_IN = 3 * RW_WIDTH + RW_DECAY_LORA + RW_AAA_LORA + RW_GATE_LORA

SB_HEADS = 8
SB_HEAD_DIM = 64
SB_WIDTH = SB_HEADS * SB_HEAD_DIM
SB_BLOCK = 128
SB_IN = 3 * SB_WIDTH

M2_HEADS = 16
M2_HEAD_DIM = 64
M2_WIDTH = M2_HEADS * M2_HEAD_DIM
M2_STATE = 128
M2_GROUPS = 2
M2_HEADS_PER_GROUP = M2_HEADS // M2_GROUPS
M2_CONV = 4
M2_CHUNK = 128
M2_CONV_DIM = M2_WIDTH + 2 * M2_GROUPS * M2_STATE
M2_IN = M2_WIDTH + M2_CONV_DIM + M2_HEADS

GATE_IN = N_BRANCH * D_MODEL
N_IN = RW_IN + SB_IN + M2_IN + GATE_IN

D_FF = 2816
FFN_CONV = 3

kernel_name = 'hybrid_rwkv7_stickbreak_mamba2_convffn'


def split_sizes(x, sizes):
    return jnp.split(x, [int(i) for i in np.cumsum(sizes)[:-1]], axis=-1)


def rms_norm(x, g):
    xf = x.astype(jnp.float32)
    y = xf * lax.rsqrt(jnp.mean(xf * xf, axis=-1, keepdims=True) + NORM_EPS)
    return (y * g.astype(jnp.float32)).astype(x.dtype)


def token_shift(x):
    return jnp.pad(x, ((0, 0), (1, 0), (0, 0)))[:, :-1]


def causal_dwconv(x, w, b):
    k_w, seq = w.shape[0], x.shape[1]
    xp = jnp.pad(x, ((0, 0), (k_w - 1, 0), (0, 0)))
    return b + sum(w[i] * xp[:, i:i + seq] for i in range(k_w))


def rwkv7_time_mix(p, mu, w0, w2, a0, a2, g2, k_k, k_a, r_k, ln_g, ln_b):
    bsz, seq, _ = p.shape
    f32 = jnp.float32
    p = p.astype(f32)
    p = p + (token_shift(p) - p) * mu
    r, k, v, w_lo, a_lo, g_lo = split_sizes(
        p, [RW_WIDTH, RW_WIDTH, RW_WIDTH, RW_DECAY_LORA, RW_AAA_LORA, RW_GATE_LORA])
    log_w = -jax.nn.softplus(-(w0 + jnp.tanh(w_lo) @ w2)) - 0.5
    decay = jnp.exp(-jnp.exp(log_w))
    a = jax.nn.sigmoid(a0 + a_lo @ a2)
    g = jax.nn.sigmoid(g_lo) @ g2
    heads = lambda t: t.reshape(bsz, seq, RW_HEADS, RW_HEAD_DIM)
    kk = heads(k * k_k)
    kk = kk * lax.rsqrt(jnp.maximum(jnp.sum(kk * kk, axis=-1, keepdims=True), 1e-24))
    k = k * (1.0 + (a - 1.0) * k_a)
    r, k, v, decay, a = heads(r), heads(k), heads(v), heads(decay), heads(a)

    def step(state, inp):
        r_t, w_t, k_t, v_t, kk_t, a_t = inp
        sa = jnp.einsum('bhij,bhj->bhi', state, -kk_t)
        state = (state * w_t[:, :, None, :]
                 + sa[..., :, None] * (kk_t * a_t)[..., None, :]
                 + v_t[..., :, None] * k_t[..., None, :])
        return state, jnp.einsum('bhij,bhj->bhi', state, r_t)

    seq_first = lambda t: jnp.swapaxes(t, 0, 1)
    init = jnp.zeros((bsz, RW_HEADS, RW_HEAD_DIM, RW_HEAD_DIM), f32)
    _, y = lax.scan(step, init, (seq_first(r), seq_first(decay), seq_first(k),
                                 seq_first(v), seq_first(kk), seq_first(a)))
    y = seq_first(y)
    mean = jnp.mean(y, axis=-1, keepdims=True)
    var = jnp.mean(jnp.square(y - mean), axis=-1, keepdims=True)
    y = ((y - mean) * lax.rsqrt(var + RW_GN_EPS)).reshape(bsz, seq, RW_WIDTH) * ln_g + ln_b
    bonus = jnp.sum(r * k * r_k, axis=-1, keepdims=True) * v
    y = y + bonus.reshape(bsz, seq, RW_WIDTH)
    return y * g


def stick_breaking_attention(q, k, v):
    bsz, seq, n_h, d_h = q.shape
    n_blk = seq // SB_BLOCK
    scale = d_h ** -0.5
    q_blocks = jnp.moveaxis(q.reshape(bsz, n_blk, SB_BLOCK, n_h, d_h), 1, 0)
    k_pos = jnp.arange(seq)

    def block(args):
        q_blk, blk_idx = args
        q_pos = blk_idx * SB_BLOCK + jnp.arange(SB_BLOCK)
        z = jnp.einsum('bqhd,bshd->bhqs', q_blk, k).astype(jnp.float32) * scale
        causal = k_pos[None, :] < q_pos[:, None]
        log_beta = jax.nn.log_sigmoid(z)
        log_1m = jnp.where(causal, jax.nn.log_sigmoid(-z), 0.0)
        later = lax.cumsum(log_1m, axis=3, reverse=True) - log_1m
        att = jnp.where(causal, jnp.exp(log_beta + later), 0.0)
        return jnp.einsum('bhqs,bshd->bqhd', att.astype(v.dtype), v)

    out = lax.map(block, (q_blocks, jnp.arange(n_blk)))
    return jnp.moveaxis(out, 0, 1).reshape(bsz, seq, n_h * d_h)


def segsum(a):
    t = a.shape[-1]
    x = jnp.broadcast_to(a[..., :, None], a.shape + (t,))
    strict = jnp.tril(jnp.ones((t, t), dtype=bool), -1)
    s = jnp.cumsum(jnp.where(strict, x, 0.0), axis=-2)
    return jnp.where(jnp.tril(jnp.ones((t, t), dtype=bool)), s, -jnp.inf)


def ssd_chunked_scan(xs, log_a, b_mat, c_mat):
    bsz, seq = xs.shape[:2]
    nc = seq // M2_CHUNK
    xs = xs.reshape(bsz, nc, M2_CHUNK, M2_GROUPS, M2_HEADS_PER_GROUP, M2_HEAD_DIM)
    b_mat = b_mat.reshape(bsz, nc, M2_CHUNK, M2_GROUPS, M2_STATE)
    c_mat = c_mat.reshape(bsz, nc, M2_CHUNK, M2_GROUPS, M2_STATE)
    log_a = log_a.reshape(bsz, nc, M2_CHUNK, M2_GROUPS, M2_HEADS_PER_GROUP).transpose(0, 3, 4, 1, 2)
    a_cum = jnp.cumsum(log_a, axis=-1)
    decay_in = jnp.exp(segsum(log_a))
    cb = jnp.einsum('bclgn,bcsgn->bgcls', c_mat, b_mat)
    y_diag = jnp.einsum('bghcls,bcsghp->bclghp', cb[:, :, None] * decay_in, xs)
    decay_to_end = jnp.exp(a_cum[..., -1:] - a_cum)
    states = jnp.einsum('bclgn,bghcl,bclghp->bcghpn', b_mat, decay_to_end, xs)
    states = jnp.concatenate([jnp.zeros_like(states[:, :1]), states], axis=1)
    chunk_decay = jnp.exp(segsum(jnp.pad(a_cum[..., -1], ((0, 0), (0, 0), (0, 0), (1, 0)))))
    states = jnp.einsum('bghzc,bcghpn->bzghpn', chunk_decay, states)[:, :-1]
    y_off = jnp.einsum('bclgn,bcghpn,bghcl->bclghp', c_mat, states, jnp.exp(a_cum))
    return (y_diag + y_off).reshape(bsz, seq, M2_HEADS, M2_HEAD_DIM)


def mamba2_mix(p, conv_w, conv_b, dt_bias, a_log, d_skip, norm_g):
    bsz, seq, _ = p.shape
    f32 = jnp.float32
    p = p.astype(f32)
    z, xbc, dt = split_sizes(p, [M2_WIDTH, M2_CONV_DIM, M2_HEADS])
    xbc = jax.nn.silu(causal_dwconv(xbc, conv_w.astype(f32), conv_b.astype(f32)))
    xs, b_mat, c_mat = split_sizes(xbc, [M2_WIDTH, M2_GROUPS * M2_STATE, M2_GROUPS * M2_STATE])
    xs = xs.reshape(bsz, seq, M2_HEADS, M2_HEAD_DIM)
    dt = jax.nn.softplus(dt + dt_bias)
    log_a = dt * -jnp.exp(a_log.astype(f32))
    grp = (bsz, seq, M2_GROUPS, M2_HEADS_PER_GROUP)
    y = ssd_chunked_scan((xs * dt[..., None]).reshape(grp + (M2_HEAD_DIM,)),
                         log_a.reshape(grp),
                         b_mat.reshape(bsz, seq, M2_GROUPS, M2_STATE),
                         c_mat.reshape(bsz, seq, M2_GROUPS, M2_STATE))
    y = y + d_skip[:, None] * xs
    y = y.reshape(bsz, seq, M2_WIDTH) * jax.nn.silu(z)
    yg = y.reshape(bsz, seq, M2_GROUPS, M2_WIDTH // M2_GROUPS)
    yg = yg * lax.rsqrt(jnp.mean(yg * yg, axis=-1, keepdims=True) + NORM_EPS)
    return yg.reshape(bsz, seq, M2_WIDTH) * norm_g


def setup_inputs(seed: int = 0) -> dict:
    key = jax.random.key(seed)
    ks = iter(jax.random.split(key, 40))
    nrm = lambda shape, scale: jax.random.normal(next(ks), shape, jnp.float32) * scale
    unif = lambda shape, lo, hi: jax.random.uniform(next(ks), shape, jnp.float32, lo, hi)
    L, D = DEPTH, D_MODEL
    dt0 = jnp.exp(unif((L, M2_HEADS), math.log(1e-3), math.log(1e-1)))
    return {
        'x': nrm((BATCH, SEQ, D), 1.0),
        'c': nrm((BATCH, D), 1.0),
        'ada_w': nrm((L, D, N_MOD * D), D ** -0.5),
        'ada_b': nrm((L, N_MOD * D), 0.02),
        'norm1_g': 1.0 + nrm((L, D), 0.02),
        'norm2_g': 1.0 + nrm((L, D), 0.02),
        'w_in': nrm((L, D, N_IN), D ** -0.5),
        'rw_mu': unif((L, RW_IN), 0.0, 1.0),
        'rw_w0': unif((L, RW_WIDTH), -7.0, -2.0),
        'rw_w2': nrm((L, RW_DECAY_LORA, RW_WIDTH), 0.5 * RW_DECAY_LORA ** -0.5),
        'rw_a0': nrm((L, RW_WIDTH), 0.1),
        'rw_a2': nrm((L, RW_AAA_LORA, RW_WIDTH), 0.5 * RW_AAA_LORA ** -0.5),
        'rw_g2': nrm((L, RW_GATE_LORA, RW_WIDTH), RW_GATE_LORA ** -0.5),
        'rw_k_k': 0.85 + nrm((L, RW_WIDTH), 0.02),
        'rw_k_a': 1.0 + nrm((L, RW_WIDTH), 0.02),
        'rw_r_k': nrm((L, RW_HEADS, RW_HEAD_DIM), 0.1),
        'rw_ln_g': 1.0 + nrm((L, RW_WIDTH), 0.02),
        'rw_ln_b': nrm((L, RW_WIDTH), 0.02),
        'rw_wo': nrm((L, RW_WIDTH, D), RW_WIDTH ** -0.5),
        'sb_wo': nrm((L, SB_WIDTH, D), SB_WIDTH ** -0.5),
        'm2_conv_w': nrm((L, M2_CONV, M2_CONV_DIM), M2_CONV ** -0.5),
        'm2_conv_b': nrm((L, M2_CONV_DIM), 0.02),
        'm2_dt_bias': dt0 + jnp.log(-jnp.expm1(-dt0)),
        'm2_a_log': jnp.log(unif((L, M2_HEADS), 1.0, 16.0)),
        'm2_d': 1.0 + nrm((L, M2_HEADS), 0.02),
        'm2_norm_g': 1.0 + nrm((L, M2_WIDTH), 0.02),
        'm2_wo': nrm((L, M2_WIDTH, D), M2_WIDTH ** -0.5),
        'w_out': nrm((L, D, D), D ** -0.5),
        'ffn_w_up': nrm((L, D, 2 * D_FF), D ** -0.5),
        'ffn_conv_w': nrm((L, FFN_CONV, 2 * D_FF), FFN_CONV ** -0.5),
        'ffn_conv_b': nrm((L, 2 * D_FF), 0.02),
        'ffn_w_down': nrm((L, D_FF, D), D_FF ** -0.5),
        'final_norm_g': 1.0 + nrm((D,), 0.02),
    }


def reference(x, c, ada_w, ada_b, norm1_g, norm2_g, w_in, rw_mu, rw_w0, rw_w2, rw_a0,
              rw_a2, rw_g2, rw_k_k, rw_k_a, rw_r_k, rw_ln_g, rw_ln_b, rw_wo, sb_wo,
              m2_conv_w, m2_conv_b, m2_dt_bias, m2_a_log, m2_d, m2_norm_g, m2_wo, w_out,
              ffn_w_up, ffn_conv_w, ffn_conv_b, ffn_w_down, final_norm_g):
    bsz, seq, _ = x.shape
    c_act = jax.nn.silu(c)
    for l in range(DEPTH):
        mod = (c_act @ ada_w[l] + ada_b[l])[:, None, :]
        shift1, scale1, gate1, shift2, scale2, gate2 = jnp.split(mod, N_MOD, axis=-1)

        h = rms_norm(x, norm1_g[l]) * (1.0 + scale1) + shift1
        p_rw, p_sb, p_m2, p_gate = split_sizes(h @ w_in[l], [RW_IN, SB_IN, M2_IN, GATE_IN])
        y_rw = rwkv7_time_mix(p_rw, rw_mu[l], rw_w0[l], rw_w2[l], rw_a0[l], rw_a2[l],
                              rw_g2[l], rw_k_k[l], rw_k_a[l], rw_r_k[l], rw_ln_g[l],
                              rw_ln_b[l]).astype(h.dtype) @ rw_wo[l]
        q, k, v = [t.reshape(bsz, seq, SB_HEADS, SB_HEAD_DIM) for t in jnp.split(p_sb, 3, axis=-1)]
        y_sb = stick_breaking_attention(q, k, v) @ sb_wo[l]
        y_m2 = mamba2_mix(p_m2, m2_conv_w[l], m2_conv_b[l], m2_dt_bias[l], m2_a_log[l],
                          m2_d[l], m2_norm_g[l]).astype(h.dtype) @ m2_wo[l]
        gates = jax.nn.sigmoid(p_gate).reshape(bsz, seq, N_BRANCH, D_MODEL)
        merged = gates[:, :, 0] * y_rw + gates[:, :, 1] * y_sb + gates[:, :, 2] * y_m2
        x = x + gate1 * (merged @ w_out[l])

        h = rms_norm(x, norm2_g[l]) * (1.0 + scale2) + shift2
        u = causal_dwconv(h @ ffn_w_up[l], ffn_conv_w[l], ffn_conv_b[l])
        u_gate, u_val = jnp.split(u, 2, axis=-1)
        x = x + gate2 * ((jax.nn.silu(u_gate) * u_val) @ ffn_w_down[l])
    return rms_norm(x, final_norm_g)
```

```python
import functools

import jax
import jax.numpy as jnp
from jax import lax
from jax.experimental import pallas as pl
from jax.experimental.pallas import tpu as pltpu

F32 = jnp.float32
BF16 = jnp.bfloat16

D_MODEL = 1024
NORM_EPS = 1e-6
N_MOD = 6

RW_HEADS = 8
HEAD_DIM = 64
RW_WIDTH = RW_HEADS * HEAD_DIM
RW_LORA = 256
RW_IN = 3 * RW_WIDTH + RW_LORA
RW_GN_EPS = 64e-5
RW_CHUNK = 64

SB_WIDTH = 512
SB_BLOCK = 128

M2_HEADS = 16
M2_WIDTH = M2_HEADS * HEAD_DIM
M2_STATE = 128
M2_GROUPS = 2
M2_GROUP_WIDTH = M2_WIDTH // M2_GROUPS
M2_CONV = 4
M2_CHUNK = 128
M2_CONV_DIM = M2_WIDTH + 2 * M2_GROUPS * M2_STATE

D_FF = 2816
FFN_CONV = 3
FFN_COL_CHUNK = 1408

LANES = 128
SUBLANES = 8

OFF_XBC = 0
OFF_SB = OFF_XBC + M2_CONV_DIM
OFF_GATE = OFF_SB + 3 * SB_WIDTH
OFF_Z = OFF_GATE + 3 * D_MODEL
OFF_RW = OFF_Z + M2_WIDTH
OFF_DT = OFF_RW + RW_IN
P_WIDTH = 9216
IN_COL_TILE = 1024

VMEM_LIMIT = 56 * 1024 * 1024


def _dot(a, b):
    return jnp.dot(a, b, preferred_element_type=F32)


def _dot_nt(a, b):
    return lax.dot_general(a, b, (((1,), (1,)), ((), ())), preferred_element_type=F32)


def _dot_tn(a, b):
    return lax.dot_general(a, b, (((0,), (0,)), ((), ())), preferred_element_type=F32)


def _split(x):
    hi = x.astype(BF16)
    lo = (x - hi.astype(F32)).astype(BF16)
    return hi, lo


def _dot_exact_rhs(x, m):
    hi, lo = _split(x)
    return _dot(hi, m) + _dot(lo, m)


def _dot_exact_lhs(m, x):
    hi, lo = _split(x)
    return _dot(m, hi) + _dot(m, lo)


def _dot_x3(a, b):
    ah, al = _split(a)
    bh, bl = _split(b)
    return _dot(ah, bh) + _dot(ah, bl) + _dot(al, bh)


def _sigmoid(x):
    return 1.0 / (1.0 + jnp.exp(-x))


def _silu(x):
    return x * _sigmoid(x)


def _softplus(x):
    return jnp.maximum(x, 0.0) + jnp.log(1.0 + jnp.exp(-jnp.abs(x)))


def _mod_kernel(c_ref, w_ref, b_ref, o_ref):
    c_act = _silu(c_ref[...]).astype(BF16)
    o_ref[0] = _dot(c_act, w_ref[0].astype(BF16)) + b_ref[0]


def _modulation(c, ada_w, ada_b):
    depth, d, n = ada_w.shape
    bsz = c.shape[0]
    tile = 1024
    return pl.pallas_call(
        _mod_kernel,
        grid=(depth, n // tile),
        in_specs=[
            pl.BlockSpec((bsz, d), lambda l, j: (0, 0)),
            pl.BlockSpec((1, d, tile), lambda l, j: (l, 0, j)),
            pl.BlockSpec((1, 1, tile), lambda l, j: (l, 0, j)),
        ],
        out_specs=pl.BlockSpec((1, bsz, tile), lambda l, j: (l, 0, j)),
        out_shape=jax.ShapeDtypeStruct((depth, bsz, n), F32),
        compiler_params=pltpu.CompilerParams(
            dimension_semantics=("parallel", "parallel"), vmem_limit_bytes=VMEM_LIMIT),
        name="ada_mod",
    )(c, ada_w, ada_b.reshape(depth, 1, n))


def _rms_mod(x, g, scale, shift):
    y = x * lax.rsqrt(jnp.mean(x * x, axis=-1, keepdims=True) + NORM_EPS) * g
    return y * (1.0 + scale) + shift


def _inproj_kernel(x_ref, mod_ref, g_ref, w_ref, o_ref, h_scr):
    @pl.when(pl.program_id(2) == 0)
    def _():
        h = _rms_mod(x_ref[0], g_ref[...], mod_ref[0, 1:2, :], mod_ref[0, 0:1, :])
        h_scr[...] = h.astype(BF16)

    o_ref[0] = _dot(h_scr[...], w_ref[...])


def _inproj(x, mod, g, wcat, tm):
    bsz, seq, d = x.shape
    return pl.pallas_call(
        _inproj_kernel,
        grid=(bsz, seq // tm, P_WIDTH // IN_COL_TILE),
        in_specs=[
            pl.BlockSpec((1, tm, d), lambda b, i, j: (b, i, 0)),
            pl.BlockSpec((1, N_MOD, d), lambda b, i, j: (b, 0, 0)),
            pl.BlockSpec((1, d), lambda b, i, j: (0, 0)),
            pl.BlockSpec((d, IN_COL_TILE), lambda b, i, j: (0, j)),
        ],
        out_specs=pl.BlockSpec((1, tm, IN_COL_TILE), lambda b, i, j: (b, i, j)),
        out_shape=jax.ShapeDtypeStruct((bsz, seq, P_WIDTH), F32),
        scratch_shapes=[pltpu.VMEM((tm, d), BF16)],
        compiler_params=pltpu.CompilerParams(
            dimension_semantics=("parallel", "parallel", "arbitrary"),
            vmem_limit_bytes=VMEM_LIMIT),
        name="inproj",
    )(x, mod, g.reshape(1, d), wcat)


def _pair_stack(x, m0):
    return jnp.concatenate([jnp.where(m0, x, 0.0), jnp.where(m0, 0.0, x)], axis=0)


def _unit_lower_inverse(a, block):
    n = a.shape[0]
    row = lax.broadcasted_iota(jnp.int32, (n, n), 0)
    col = lax.broadcasted_iota(jnp.int32, (n, n), 1)

    def quadrant(l):
        same = (row >> l) == (col >> l)
        low = ((row >> (l - 1)) & 1) == 1
        left = ((col >> (l - 1)) & 1) == 0
        return same & low & left

    inv = jnp.where(row == col, 1.0, 0.0) + jnp.where(quadrant(1), a, 0.0)
    level = 2
    while (1 << level) <= block:
        a_l = jnp.where(quadrant(level), a, 0.0)
        inv = inv + _dot_x3(inv, _dot_x3(a_l, inv))
        level += 1
    return inv


def _rwkv_kernel(p_ref, mu_ref, wl_ref, vec_ref, bd_ref, o_ref, pbuf, state):
    cn = RW_CHUNK
    c = pl.program_id(1)

    @pl.when(c == 0)
    def _():
        pbuf[0:SUBLANES, :] = jnp.zeros((SUBLANES, RW_IN), F32)
        state[...] = jnp.zeros_like(state)

    p = p_ref[0]
    pbuf[SUBLANES:SUBLANES + cn, :] = p
    prev = pbuf[SUBLANES - 1:SUBLANES - 1 + cn, :]
    pbuf[0:SUBLANES, :] = p[cn - SUBLANES:cn, :]
    pm = p + (prev - p) * mu_ref[...]

    w0 = vec_ref[0:1, :]
    a0 = vec_ref[1:2, :]
    k_k = vec_ref[2:3, :]
    k_a = vec_ref[3:4, :]
    r_k = vec_ref[4:5, :]
    ln_g = vec_ref[5:6, :]
    ln_b = vec_ref[6:7, :]
    bd = bd_ref[...]

    r = pm[:, 0:RW_WIDTH]
    k = pm[:, RW_WIDTH:2 * RW_WIDTH]
    v = pm[:, 2 * RW_WIDTH:3 * RW_WIDTH]
    lo = pm[:, 3 * RW_WIDTH:RW_IN]
    lane = lax.broadcasted_iota(jnp.int32, lo.shape, 1)
    act = jnp.where(lane < 64, jnp.tanh(lo), jnp.where(lane < 128, lo, _sigmoid(lo)))
    proj = _dot(act.astype(BF16), wl_ref[...])

    log_w = -_softplus(-(w0 + proj[:, 0:RW_WIDTH])) - 0.5
    ld = -jnp.exp(log_w)
    a = _sigmoid(a0 + proj[:, RW_WIDTH:2 * RW_WIDTH])
    g = proj[:, 2 * RW_WIDTH:3 * RW_WIDTH]

    kk = k * k_k
    kk = kk * lax.rsqrt(jnp.maximum(_dot_exact_rhs(kk * kk, bd), 1e-24))
    k = k * (1.0 + (a - 1.0) * k_a)

    trow = lax.broadcasted_iota(jnp.int32, (cn, cn), 0)
    tcol = lax.broadcasted_iota(jnp.int32, (cn, cn), 1)
    ltri = jnp.where(trow >= tcol, 1.0, 0.0).astype(BF16)
    cum = _dot_exact_lhs(ltri, ld)
    g_incl = jnp.exp(cum)
    g_inv = jnp.exp(-cum)
    al_bar = -kk * jnp.exp(cum - ld)
    r_bar = r * g_incl
    be_t = kk * a * g_inv
    k_t = k * g_inv
    g_end = jnp.exp(cum[cn - 1:cn, :])

    m0 = lax.broadcasted_iota(jnp.int32, (cn, LANES), 1) < HEAD_DIM
    n2 = 2 * cn
    srow = lax.broadcasted_iota(jnp.int32, (n2, n2), 0)
    scol = lax.broadcasted_iota(jnp.int32, (n2, n2), 1)
    strict = srow > scol
    incl = srow >= scol

    ys = []
    for q in range(RW_WIDTH // LANES):
        sl = slice(q * LANES, (q + 1) * LANES)
        al = _pair_stack(al_bar[:, sl], m0)
        rb = _pair_stack(r_bar[:, sl], m0)
        bt = _pair_stack(be_t[:, sl], m0)
        kt = _pair_stack(k_t[:, sl], m0)
        vs = _pair_stack(v[:, sl], m0)
        lhs = jnp.concatenate([al, rb], axis=0).astype(BF16)
        rhs = jnp.concatenate([bt, kt], axis=0)
        gram = _dot_nt(lhs, rhs.astype(BF16))
        a_mat = jnp.where(strict, gram[0:n2, 0:n2], 0.0)
        b_mat = jnp.where(strict, gram[0:n2, n2:2 * n2], 0.0)
        arb_ark = jnp.concatenate(
            [jnp.where(incl, gram[n2:2 * n2, 0:n2], 0.0),
             jnp.where(incl, gram[n2:2 * n2, n2:2 * n2], 0.0)], axis=1)
        s_q = state[q]
        from_state = _dot_nt(lhs, s_q.astype(BF16))
        vs_b = vs.astype(BF16)
        u = _dot_x3(_unit_lower_inverse(a_mat, cn),
                    from_state[0:n2] + _dot(b_mat.astype(BF16), vs_b))
        uv = jnp.concatenate([u.astype(BF16), vs_b], axis=0)
        y_ps = from_state[n2:2 * n2] + _dot(arb_ark.astype(BF16), uv)
        ys.append(y_ps[0:cn] + y_ps[cn:n2])
        ge = g_end[:, sl]
        state[q] = s_q * ge + _dot_tn(uv, (rhs * ge).astype(BF16))
    y = jnp.concatenate(ys, axis=1)

    inv_n = 1.0 / HEAD_DIM
    mean = _dot_exact_rhs(y, bd) * inv_n
    yc = y - mean
    var = _dot_exact_rhs(yc * yc, bd) * inv_n
    y = yc * lax.rsqrt(var + RW_GN_EPS) * ln_g + ln_b
    y = y + _dot_exact_rhs(r * k * r_k, bd) * v
    o_ref[0] = y * g


def _rwkv(p, mu, wl, vecs, bd):
    bsz, seq, _ = p.shape
    cn = RW_CHUNK
    const = lambda b, c: (0, 0)
    return pl.pallas_call(
        _rwkv_kernel,
        grid=(bsz, seq // cn),
        in_specs=[
            pl.BlockSpec((1, cn, RW_IN), lambda b, c: (b, c, OFF_RW // RW_IN)),
            pl.BlockSpec((1, RW_IN), const),
            pl.BlockSpec((RW_LORA, 3 * RW_WIDTH), const),
            pl.BlockSpec((SUBLANES, RW_WIDTH), const),
            pl.BlockSpec((RW_WIDTH, RW_WIDTH), const),
        ],
        out_specs=pl.BlockSpec((1, cn, RW_WIDTH), lambda b, c: (b, c, 0)),
        out_shape=jax.ShapeDtypeStruct((bsz, seq, RW_WIDTH), F32),
        scratch_shapes=[
            pltpu.VMEM((SUBLANES + cn, RW_IN), F32),
            pltpu.VMEM((RW_WIDTH // LANES, LANES, LANES), F32),
        ],
        compiler_params=pltpu.CompilerParams(
            dimension_semantics=("parallel", "arbitrary"), vmem_limit_bytes=VMEM_LIMIT),
        name="rwkv7",
    )(p, mu, wl, vecs, bd)


def _sb_kernel(q_ref, k_ref, v_ref, o_ref):
    tb = SB_BLOCK
    qi = pl.program_id(2)
    q = q_ref[0] * (HEAD_DIM ** -0.5)
    m0 = lax.broadcasted_iota(jnp.int32, (tb, LANES), 1) < HEAD_DIM
    q_heads = (jnp.where(m0, q, 0.0).astype(BF16), jnp.where(m0, 0.0, q).astype(BF16))
    row = lax.broadcasted_iota(jnp.int32, (tb, tb), 0)
    col = lax.broadcasted_iota(jnp.int32, (tb, tb), 1)
    causal = col < row
    later_keys = jnp.where(row > col, 1.0, 0.0).astype(BF16)

    def block(j, carry, diag):
        start = pl.multiple_of(j * tb, tb)
        k_blk = k_ref[0, pl.ds(start, tb), :].astype(BF16)
        v_blk = v_ref[0, pl.ds(start, tb), :].astype(BF16)
        out = []
        for h in range(2):
            acc, run = carry[h]
            z = _dot_nt(q_heads[h], k_blk)
            log_beta = jnp.minimum(z, 0.0) - jnp.log(1.0 + jnp.exp(-jnp.abs(z)))
            log_1m = log_beta - z
            if diag:
                log_1m = jnp.where(causal, log_1m, 0.0)
            later = _dot_exact_rhs(log_1m, later_keys) + run
            att = jnp.exp(log_beta + later)
            if diag:
                att = jnp.where(causal, att, 0.0)
            acc = acc + _dot(att.astype(BF16), v_blk)
            run = run + jnp.sum(log_1m, axis=-1, keepdims=True)
            out.append((acc, run))
        return tuple(out)

    zero = (jnp.zeros((tb, LANES), F32), jnp.zeros((tb, 1), F32))
    carry = block(qi, (zero, zero), True)
    carry = lax.fori_loop(0, qi, lambda i, cr: block(qi - 1 - i, cr, False), carry)
    o_ref[0] = jnp.where(m0, carry[0][0], carry[1][0])


def _sb_attention(p):
    bsz, seq, _ = p.shape
    tb = SB_BLOCK
    q0 = OFF_SB // LANES
    k0 = (OFF_SB + SB_WIDTH) // LANES
    v0 = (OFF_SB + 2 * SB_WIDTH) // LANES
    return pl.pallas_call(
        _sb_kernel,
        grid=(bsz, SB_WIDTH // LANES, seq // tb),
        in_specs=[
            pl.BlockSpec((1, tb, LANES), lambda b, h, i: (b, i, q0 + h)),
            pl.BlockSpec((1, seq, LANES), lambda b, h, i: (b, 0, k0 + h)),
            pl.BlockSpec((1, seq, LANES), lambda b, h, i: (b, 0, v0 + h)),
        ],
        out_specs=pl.BlockSpec((1, tb, LANES), lambda b, h, i: (b, i, h)),
        out_shape=jax.ShapeDtypeStruct((bsz, seq, SB_WIDTH), F32),
        compiler_params=pltpu.CompilerParams(
            dimension_semantics=("parallel", "parallel", "arbitrary"),
            vmem_limit_bytes=VMEM_LIMIT),
        name="stickbreak",
    )(p, p, p)


def _mamba_kernel(xbc_ref, z_ref, dt_ref, cw_ref, cb_ref, hv_ref, dn_ref, ex_ref, o_ref,
                  xpad, state):
    ln = M2_CHUNK
    c = pl.program_id(1)

    @pl.when(c == 0)
    def _():
        xpad[0:SUBLANES, :] = jnp.zeros((SUBLANES, M2_CONV_DIM), F32)
        state[...] = jnp.zeros_like(state)

    x = xbc_ref[0]
    xpad[SUBLANES:SUBLANES + ln, :] = x
    conv = cb_ref[...] + cw_ref[M2_CONV - 1:M2_CONV, :] * x
    for i in range(M2_CONV - 1):
        sh = M2_CONV - 1 - i
        conv = conv + cw_ref[i:i + 1, :] * xpad[SUBLANES - sh:SUBLANES - sh + ln, :]
    xpad[0:SUBLANES, :] = x[ln - SUBLANES:ln, :]
    xa = _silu(conv)
    xs = xa[:, 0:M2_WIDTH]

    dt = _softplus(dt_ref[0] + hv_ref[0:1, :])
    log_a = dt * (-jnp.exp(hv_ref[1:2, :]))
    row = lax.broadcasted_iota(jnp.int32, (ln, ln), 0)
    col = lax.broadcasted_iota(jnp.int32, (ln, ln), 1)
    lower = row >= col
    ltri = jnp.where(lower, 1.0, 0.0).astype(BF16)
    acum = _dot_exact_lhs(ltri, log_a)
    acum_t = acum.T
    ex = ex_ref[...]
    dt_x = _dot_exact_rhs(dt, ex)
    acum_x = _dot_exact_rhs(acum, ex)
    xdt = xs * dt_x

    m0 = lax.broadcasted_iota(jnp.int32, (ln, LANES), 1) < HEAD_DIM
    gw = M2_GROUP_WIDTH
    heads_per_group = M2_HEADS // M2_GROUPS
    y_groups = []
    for gi in range(M2_GROUPS):
        b_g = xa[:, M2_WIDTH + gi * M2_STATE:M2_WIDTH + (gi + 1) * M2_STATE].astype(BF16)
        c_off = M2_WIDTH + M2_GROUPS * M2_STATE
        c_g = xa[:, c_off + gi * M2_STATE:c_off + (gi + 1) * M2_STATE].astype(BF16)
        cb = _dot_nt(c_g, b_g)
        gs = slice(gi * gw, (gi + 1) * gw)
        ac_g = acum_x[:, gs]
        st = state[gi]
        y_off = _dot(c_g, st.astype(BF16)) * jnp.exp(ac_g)
        pairs = []
        for q in range(heads_per_group // 2):
            mats = []
            for hh in range(2):
                h = gi * heads_per_group + 2 * q + hh
                seg = acum[:, h:h + 1] - acum_t[h:h + 1, :]
                mats.append(cb * jnp.exp(jnp.where(lower, seg, -1e30)))
            xp = xdt[:, gi * gw + q * LANES:gi * gw + (q + 1) * LANES]
            pairs.append(_dot(jnp.concatenate(mats, axis=1).astype(BF16),
                              _pair_stack(xp, m0).astype(BF16)))
        y_diag = jnp.concatenate(pairs, axis=1)
        ac_end = ac_g[ln - 1:ln, :]
        to_end = jnp.exp(ac_end - ac_g)
        state[gi] = st * jnp.exp(ac_end) + _dot_tn(b_g, (xdt[:, gs] * to_end).astype(BF16))
        y = y_diag + y_off + dn_ref[0:1, gs] * xs[:, gs]
        y = y * _silu(z_ref[0][:, gs])
        y = y * lax.rsqrt(jnp.mean(y * y, axis=-1, keepdims=True) + NORM_EPS)
        y_groups.append(y * dn_ref[1:2, gs])
    o_ref[0] = jnp.concatenate(y_groups, axis=1)


def _mamba(p, conv_w, conv_b, head_vecs, dn_vecs, expand):
    bsz, seq, _ = p.shape
    ln = M2_CHUNK
    const = lambda b, c: (0, 0)
    return pl.pallas_call(
        _mamba_kernel,
        grid=(bsz, seq // ln),
        in_specs=[
            pl.BlockSpec((1, ln, M2_CONV_DIM), lambda b, c: (b, c, OFF_XBC // M2_CONV_DIM)),
            pl.BlockSpec((1, ln, M2_WIDTH), lambda b, c: (b, c, OFF_Z // M2_WIDTH)),
            pl.BlockSpec((1, ln, LANES), lambda b, c: (b, c, OFF_DT // LANES)),
            pl.BlockSpec((M2_CONV, M2_CONV_DIM), const),
            pl.BlockSpec((1, M2_CONV_DIM), const),
            pl.BlockSpec((SUBLANES, LANES), const),
            pl.BlockSpec((SUBLANES, M2_WIDTH), const),
            pl.BlockSpec((LANES, M2_WIDTH), const),
        ],
        out_specs=pl.BlockSpec((1, ln, M2_WIDTH), lambda b, c: (b, c, 0)),
        out_shape=jax.ShapeDtypeStruct((bsz, seq, M2_WIDTH), F32),
        scratch_shapes=[
            pltpu.VMEM((SUBLANES + ln, M2_CONV_DIM), F32),
            pltpu.VMEM((M2_GROUPS, M2_STATE, M2_GROUP_WIDTH), F32),
        ],
        compiler_params=pltpu.CompilerParams(
            dimension_semantics=("parallel", "arbitrary"), vmem_limit_bytes=VMEM_LIMIT),
        name="mamba2",
    )(p, p, p, conv_w, conv_b, head_vecs, dn_vecs, expand)


def _merge_kernel(yrw_ref, ysb_ref, ym2_ref, pg_ref, x_ref, mod_ref, g2_ref,
                  wrw_ref, wsb_ref, wm2_ref, wo_ref, xo_ref, h_ref):
    d = D_MODEL
    pg = pg_ref[0]
    merged = _sigmoid(pg[:, 0:d]) * _dot(yrw_ref[0].astype(BF16), wrw_ref[...])
    merged = merged + _sigmoid(pg[:, d:2 * d]) * _dot(ysb_ref[0].astype(BF16), wsb_ref[...])
    merged = merged + _sigmoid(pg[:, 2 * d:3 * d]) * _dot(ym2_ref[0].astype(BF16), wm2_ref[...])
    xn = x_ref[0] + mod_ref[0, 2:3, :] * _dot(merged.astype(BF16), wo_ref[...])
    xo_ref[0] = xn
    h_ref[0] = _rms_mod(xn, g2_ref[...], mod_ref[0, 4:5, :], mod_ref[0, 3:4, :]).astype(BF16)


def _merge(y_rw, y_sb, y_m2, p, x, mod, g2, w_rw, w_sb, w_m2, w_o, tm):
    bsz, seq, d = x.shape
    const = lambda b, i: (0, 0)
    tok = lambda w: pl.BlockSpec((1, tm, w), lambda b, i: (b, i, 0))
    return pl.pallas_call(
        _merge_kernel,
        grid=(bsz, seq // tm),
        in_specs=[
            tok(RW_WIDTH), tok(SB_WIDTH), tok(M2_WIDTH),
            pl.BlockSpec((1, tm, 3 * d), lambda b, i: (b, i, OFF_GATE // (3 * d))),
            tok(d),
            pl.BlockSpec((1, N_MOD, d), lambda b, i: (b, 0, 0)),
            pl.BlockSpec((1, d), const),
            pl.BlockSpec((RW_WIDTH, d), const),
            pl.BlockSpec((SB_WIDTH, d), const),
            pl.BlockSpec((M2_WIDTH, d), const),
            pl.BlockSpec((d, d), const),
        ],
        out_specs=[tok(d), tok(d)],
        out_shape=[jax.ShapeDtypeStruct((bsz, seq, d), F32),
                   jax.ShapeDtypeStruct((bsz, seq, d), BF16)],
        compiler_params=pltpu.CompilerParams(
            dimension_semantics=("parallel", "parallel"), vmem_limit_bytes=VMEM_LIMIT),
        name="merge",
    )(y_rw, y_sb, y_m2, p, x, mod, g2.reshape(1, d), w_rw, w_sb, w_m2, w_o)


def _ffn_kernel(h_ref, x_ref, mod_ref, wu_ref, cw_ref, cb_ref, wd_ref, fg_ref, o_ref,
                ubuf, carry, *, final):
    tm = h_ref.shape[1]
    fc = FFN_COL_CHUNK

    @pl.when(pl.program_id(1) == 0)
    def _():
        carry[...] = jnp.zeros_like(carry)

    h = h_ref[0]
    acc = jnp.zeros((tm, D_MODEL), F32)
    for ci in range(D_FF // fc):
        halves = []
        for half in range(2):
            cols = slice(half * D_FF + ci * fc, half * D_FF + (ci + 1) * fc)
            u = _dot(h, wu_ref[:, cols])
            ubuf[0:SUBLANES, :] = carry[:, cols]
            ubuf[SUBLANES:SUBLANES + tm, :] = u
            carry[:, cols] = u[tm - SUBLANES:tm, :]
            conv = cb_ref[:, cols] + cw_ref[FFN_CONV - 1:FFN_CONV, cols] * u
            for i in range(FFN_CONV - 1):
                sh = FFN_CONV - 1 - i
                conv = conv + cw_ref[i:i + 1, cols] * ubuf[SUBLANES - sh:SUBLANES - sh + tm, :]
            halves.append(conv)
        act = _silu(halves[0]) * halves[1]
        acc = acc + _dot(act.astype(BF16), wd_ref[ci * fc:(ci + 1) * fc, :])
    xn = x_ref[0] + mod_ref[0, 5:6, :] * acc
    if final:
        xn = xn * lax.rsqrt(jnp.mean(xn * xn, axis=-1, keepdims=True) + NORM_EPS) * fg_ref[...]
    o_ref[0] = xn


def _ffn(h2, x, mod, w_up, conv_w, conv_b, w_down, final_g, tm, final):
    bsz, seq, d = x.shape
    const = lambda b, i: (0, 0)
    tok = pl.BlockSpec((1, tm, d), lambda b, i: (b, i, 0))
    resident = lambda shape: pl.BlockSpec(shape, const, pipeline_mode=pl.Buffered(1))
    return pl.pallas_call(
        functools.partial(_ffn_kernel, final=final),
        grid=(bsz, seq // tm),
        in_specs=[
            tok, tok,
            pl.BlockSpec((1, N_MOD, d), lambda b, i: (b, 0, 0)),
            resident((d, 2 * D_FF)),
            pl.BlockSpec((FFN_CONV, 2 * D_FF), const),
            pl.BlockSpec((1, 2 * D_FF), const),
            resident((D_FF, d)),
            pl.BlockSpec((1, d), const),
        ],
        out_specs=tok,
        out_shape=jax.ShapeDtypeStruct((bsz, seq, d), F32),
        scratch_shapes=[
            pltpu.VMEM((SUBLANES + tm, FFN_COL_CHUNK), F32),
            pltpu.VMEM((SUBLANES, 2 * D_FF), F32),
        ],
        compiler_params=pltpu.CompilerParams(
            dimension_semantics=("parallel", "arbitrary"), vmem_limit_bytes=VMEM_LIMIT),
        name="convffn",
    )(h2, x, mod, w_up, conv_w, conv_b.reshape(1, -1), w_down, final_g.reshape(1, d))


def _in_weight(w_in):
    rw, sb = RW_IN, 3 * SB_WIDTH
    z0 = rw + sb
    x0 = z0 + M2_WIDTH
    t0 = x0 + M2_CONV_DIM
    g0 = t0 + M2_HEADS
    d = w_in.shape[0]
    parts = [w_in[:, x0:t0], w_in[:, rw:z0], w_in[:, g0:g0 + 3 * D_MODEL], w_in[:, z0:x0],
             w_in[:, 0:rw], w_in[:, t0:g0],
             jnp.zeros((d, P_WIDTH - OFF_DT - M2_HEADS), w_in.dtype)]
    return jnp.concatenate(parts, axis=1).astype(BF16)


def _rwkv_lora_weight(w2, a2, g2):
    wl = jnp.zeros((RW_LORA, 3 * RW_WIDTH), F32)
    wl = wl.at[0:64, 0:RW_WIDTH].set(w2)
    wl = wl.at[64:128, RW_WIDTH:2 * RW_WIDTH].set(a2)
    wl = wl.at[128:256, 2 * RW_WIDTH:3 * RW_WIDTH].set(g2)
    return wl.astype(BF16)


def _rows(vectors, width, rows=SUBLANES):
    out = jnp.zeros((rows, width), F32)
    for i, vec in enumerate(vectors):
        out = out.at[i, 0:vec.shape[0]].set(vec.astype(F32))
    return out


def kernel(x, c, ada_w, ada_b, norm1_g, norm2_g, w_in, rw_mu, rw_w0, rw_w2, rw_a0, rw_a2,
           rw_g2, rw_k_k, rw_k_a, rw_r_k, rw_ln_g, rw_ln_b, rw_wo, sb_wo, m2_conv_w,
           m2_conv_b, m2_dt_bias, m2_a_log, m2_d, m2_norm_g, m2_wo, w_out, ffn_w_up,
           ffn_conv_w, ffn_conv_b, ffn_w_down, final_norm_g):
    bsz, seq, d = x.shape
    depth = ada_w.shape[0]
    assert d == D_MODEL and seq % 128 == 0
    tm_in = min(1024, seq)
    tm_merge = min(512, seq)
    tm_ffn = min(256, seq)

    head = jnp.arange(RW_WIDTH) // HEAD_DIM
    same_head = (head[:, None] == head[None, :]).astype(BF16)
    lane_head = jnp.arange(M2_WIDTH) // HEAD_DIM
    expand = (jnp.arange(LANES)[:, None] == lane_head[None, :]).astype(BF16)

    mod_all = _modulation(c, ada_w, ada_b)
    for l in range(depth):
        mod = mod_all[l].reshape(bsz, N_MOD, d)
        p = _inproj(x, mod, norm1_g[l], _in_weight(w_in[l]), tm_in)
        rw_vecs = _rows([rw_w0[l], rw_a0[l], rw_k_k[l], rw_k_a[l], rw_r_k[l].reshape(-1),
                         rw_ln_g[l], rw_ln_b[l]], RW_WIDTH)
        y_rw = _rwkv(p, rw_mu[l].reshape(1, RW_IN),
                     _rwkv_lora_weight(rw_w2[l], rw_a2[l], rw_g2[l]), rw_vecs, same_head)
        y_sb = _sb_attention(p)
        head_vecs = _rows([m2_dt_bias[l], m2_a_log[l]], LANES)
        dn_vecs = _rows([jnp.repeat(m2_d[l], HEAD_DIM), m2_norm_g[l]], M2_WIDTH)
        y_m2 = _mamba(p, m2_conv_w[l], m2_conv_b[l].reshape(1, -1), head_vecs, dn_vecs, expand)
        x, h2 = _merge(y_rw, y_sb, y_m2, p, x, mod, norm2_g[l], rw_wo[l].astype(BF16),
                       sb_wo[l].astype(BF16), m2_wo[l].astype(BF16), w_out[l].astype(BF16),
                       tm_merge)
        x = _ffn(h2, x, mod, ffn_w_up[l].astype(BF16), ffn_conv_w[l], ffn_conv_b[l],
                 ffn_w_down[l].astype(BF16), final_norm_g, tm_ffn, final=(l == depth - 1))
    return x
```

```python
import functools

import jax
import jax.numpy as jnp
from jax import lax
from jax.experimental import pallas as pl
from jax.experimental.pallas import tpu as pltpu

F32 = jnp.float32
BF16 = jnp.bfloat16

D_MODEL = 1024
NORM_EPS = 1e-6
N_MOD = 6

RW_HEADS = 8
HEAD_DIM = 64
RW_WIDTH = RW_HEADS * HEAD_DIM
RW_LORA = 256
RW_IN = 3 * RW_WIDTH + RW_LORA
RW_GN_EPS = 64e-5
RW_CHUNK = 64
RW_STEP_CHUNKS = 2

SB_WIDTH = 512
SB_BLOCK = 256
SB_LANES = 256

M2_HEADS = 16
M2_WIDTH = M2_HEADS * HEAD_DIM
M2_STATE = 128
M2_GROUPS = 2
M2_GROUP_WIDTH = M2_WIDTH // M2_GROUPS
M2_CONV = 4
M2_CHUNK = 128
M2_CONV_DIM = M2_WIDTH + 2 * M2_GROUPS * M2_STATE

D_FF = 2816
FFN_CONV = 3
FFN_COL_CHUNK = 1408

LANES = 128
SUBLANES = 8

OFF_XBC = 0
OFF_SB = OFF_XBC + M2_CONV_DIM
OFF_GATE = OFF_SB + 3 * SB_WIDTH
OFF_Z = OFF_GATE + 3 * D_MODEL
OFF_RW = OFF_Z + M2_WIDTH
OFF_DT = OFF_RW + RW_IN
P_WIDTH = 9216
IN_COL_TILE = 1024

VMEM_LIMIT = 56 * 1024 * 1024


def _dot(a, b):
    return jnp.dot(a, b, preferred_element_type=F32)


def _dot_nt(a, b):
    return lax.dot_general(a, b, (((1,), (1,)), ((), ())), preferred_element_type=F32)


def _dot_tn(a, b):
    return lax.dot_general(a, b, (((0,), (0,)), ((), ())), preferred_element_type=F32)


def _split(x):
    hi = x.astype(BF16)
    lo = (x - hi.astype(F32)).astype(BF16)
    return hi, lo


def _dot_exact_rhs(x, m):
    hi, lo = _split(x)
    return _dot(hi, m) + _dot(lo, m)


def _dot_exact_lhs(m, x):
    hi, lo = _split(x)
    return _dot(m, hi) + _dot(m, lo)


def _sigmoid(x):
    return 1.0 / (1.0 + jnp.exp(-x))


def _silu(x):
    return x * _sigmoid(x)


def _softplus(x):
    return jnp.maximum(x, 0.0) + jnp.log(1.0 + jnp.exp(-jnp.abs(x)))


def _mod_kernel(c_ref, w_ref, b_ref, o_ref):
    c_act = _silu(c_ref[...]).astype(BF16)
    o_ref[0] = _dot(c_act, w_ref[0].astype(BF16)) + b_ref[0]


def _modulation(c, ada_w, ada_b):
    depth, d, n = ada_w.shape
    bsz = c.shape[0]
    tile = 1024
    return pl.pallas_call(
        _mod_kernel,
        grid=(depth, n // tile),
        in_specs=[
            pl.BlockSpec((bsz, d), lambda l, j: (0, 0)),
            pl.BlockSpec((1, d, tile), lambda l, j: (l, 0, j)),
            pl.BlockSpec((1, 1, tile), lambda l, j: (l, 0, j)),
        ],
        out_specs=pl.BlockSpec((1, bsz, tile), lambda l, j: (l, 0, j)),
        out_shape=jax.ShapeDtypeStruct((depth, bsz, n), F32),
        compiler_params=pltpu.CompilerParams(
            dimension_semantics=("parallel", "parallel"), vmem_limit_bytes=VMEM_LIMIT),
        name="ada_mod",
    )(c, ada_w, ada_b.reshape(depth, 1, n))


def _rms_mod(x, g, scale, shift):
    y = x * lax.rsqrt(jnp.mean(x * x, axis=-1, keepdims=True) + NORM_EPS) * g
    return y * (1.0 + scale) + shift


def _inproj_kernel(x_ref, mod_ref, g_ref, w_ref, o_ref, h_scr):
    @pl.when(pl.program_id(2) == 0)
    def _():
        h = _rms_mod(x_ref[0], g_ref[...], mod_ref[0, 1:2, :], mod_ref[0, 0:1, :])
        h_scr[...] = h.astype(BF16)

    o_ref[0] = _dot(h_scr[...], w_ref[...])


def _inproj(x, mod, g, wcat, tm):
    bsz, seq, d = x.shape
    return pl.pallas_call(
        _inproj_kernel,
        grid=(bsz, seq // tm, P_WIDTH // IN_COL_TILE),
        in_specs=[
            pl.BlockSpec((1, tm, d), lambda b, i, j: (b, i, 0)),
            pl.BlockSpec((1, N_MOD, d), lambda b, i, j: (b, 0, 0)),
            pl.BlockSpec((1, d), lambda b, i, j: (0, 0)),
            pl.BlockSpec((d, IN_COL_TILE), lambda b, i, j: (0, j)),
        ],
        out_specs=pl.BlockSpec((1, tm, IN_COL_TILE), lambda b, i, j: (b, i, j)),
        out_shape=jax.ShapeDtypeStruct((bsz, seq, P_WIDTH), F32),
        scratch_shapes=[pltpu.VMEM((tm, d), BF16)],
        compiler_params=pltpu.CompilerParams(
            dimension_semantics=("parallel", "parallel", "arbitrary"),
            vmem_limit_bytes=VMEM_LIMIT),
        name="inproj",
    )(x, mod, g.reshape(1, d), wcat)


def _pair_stack(x, m0):
    return jnp.concatenate([jnp.where(m0, x, 0.0), jnp.where(m0, 0.0, x)], axis=0)


def _unit_lower_inverse(mats, block):
    n = mats[0].shape[0]
    row = lax.broadcasted_iota(jnp.int32, (n, n), 0)
    col = lax.broadcasted_iota(jnp.int32, (n, n), 1)

    def quadrant(l):
        same = (row >> l) == (col >> l)
        low = ((row >> (l - 1)) & 1) == 1
        left = ((col >> (l - 1)) & 1) == 0
        return same & low & left

    eye = jnp.where(row == col, 1.0, 0.0)
    first = quadrant(1)
    inv = [eye + jnp.where(first, a, 0.0) for a in mats]
    level = 2
    while (1 << level) <= block:
        quad = quadrant(level)
        inv_b = [x.astype(BF16) for x in inv]
        t = [_dot(jnp.where(quad, a, 0.0).astype(BF16), xb) for a, xb in zip(mats, inv_b)]
        inv = [x + _dot(xb, tt.astype(BF16)) for x, xb, tt in zip(inv, inv_b, t)]
        level += 1
    return inv


def _rwkv_kernel(p_ref, mu_ref, wl_ref, vec_ref, bd_ref, o_ref, pbuf, state):
    cn = RW_CHUNK
    tb = RW_STEP_CHUNKS * cn
    n_pairs = RW_WIDTH // LANES
    c = pl.program_id(1)

    @pl.when(c == 0)
    def _():
        pbuf[0:SUBLANES, :] = jnp.zeros((SUBLANES, RW_IN), F32)
        state[...] = jnp.zeros_like(state)

    p = p_ref[0]
    pbuf[SUBLANES:SUBLANES + tb, :] = p
    prev = pbuf[SUBLANES - 1:SUBLANES - 1 + tb, :]
    pbuf[0:SUBLANES, :] = p[tb - SUBLANES:tb, :]
    pm = p + (prev - p) * mu_ref[...]

    w0 = vec_ref[0:1, :]
    a0 = vec_ref[1:2, :]
    k_k = vec_ref[2:3, :]
    k_a = vec_ref[3:4, :]
    r_k = vec_ref[4:5, :]
    ln_g = vec_ref[5:6, :]
    ln_b = vec_ref[6:7, :]
    bd = bd_ref[...]

    r = pm[:, 0:RW_WIDTH]
    k = pm[:, RW_WIDTH:2 * RW_WIDTH]
    v = pm[:, 2 * RW_WIDTH:3 * RW_WIDTH]
    lo = pm[:, 3 * RW_WIDTH:RW_IN]
    lane = lax.broadcasted_iota(jnp.int32, lo.shape, 1)
    act = jnp.where(lane < 64, jnp.tanh(lo), jnp.where(lane < 128, lo, _sigmoid(lo)))
    proj = _dot(act.astype(BF16), wl_ref[...])

    log_w = -_softplus(-(w0 + proj[:, 0:RW_WIDTH])) - 0.5
    ld = -jnp.exp(log_w)
    a = _sigmoid(a0 + proj[:, RW_WIDTH:2 * RW_WIDTH])
    g = proj[:, 2 * RW_WIDTH:3 * RW_WIDTH]

    kk = k * k_k
    kk = kk * lax.rsqrt(jnp.maximum(_dot_exact_rhs(kk * kk, bd), 1e-24))
    k = k * (1.0 + (a - 1.0) * k_a)

    trow = lax.broadcasted_iota(jnp.int32, (tb, tb), 0)
    tcol = lax.broadcasted_iota(jnp.int32, (tb, tb), 1)
    shift = cn.bit_length() - 1
    same_chunk = (trow >> shift) == (tcol >> shift)
    ltri = jnp.where((trow >= tcol) & same_chunk, 1.0, 0.0).astype(BF16)
    cum = _dot_exact_lhs(ltri, ld)
    g_incl = jnp.exp(cum)
    g_inv = jnp.exp(-cum)
    al_bar = -kk * jnp.exp(cum - ld)
    r_bar = r * g_incl
    be_t = kk * a * g_inv
    k_t = k * g_inv

    m0 = lax.broadcasted_iota(jnp.int32, (cn, LANES), 1) < HEAD_DIM
    n2 = 2 * cn
    srow = lax.broadcasted_iota(jnp.int32, (n2, n2), 0)
    scol = lax.broadcasted_iota(jnp.int32, (n2, n2), 1)
    strict = srow > scol
    incl = srow >= scol

    units = [(ci, q) for ci in range(RW_STEP_CHUNKS) for q in range(n_pairs)]

    def stacked(x, ci, q):
        return _pair_stack(x[ci * cn:(ci + 1) * cn, q * LANES:(q + 1) * LANES], m0)

    al_b = [stacked(al_bar, ci, q).astype(BF16) for ci, q in units]
    rb = [stacked(r_bar, ci, q) for ci, q in units]
    vs_b = [stacked(v, ci, q).astype(BF16) for ci, q in units]
    rhs = [jnp.concatenate([stacked(be_t, ci, q), stacked(k_t, ci, q)], axis=0)
           for ci, q in units]
    g_end = [jnp.exp(cum[(ci + 1) * cn - 1:(ci + 1) * cn, q * LANES:(q + 1) * LANES])
             for ci, q in units]
    gram = [_dot_nt(jnp.concatenate([x, y.astype(BF16)], axis=0), z.astype(BF16))
            for x, y, z in zip(al_b, rb, rhs)]
    inv = _unit_lower_inverse([jnp.where(strict, gm[0:n2, 0:n2], 0.0) for gm in gram], cn)
    inv_b = [x.astype(BF16) for x in inv]
    a_til = [_dot(nb, x) for nb, x in zip(inv_b, al_b)]
    bv = [_dot(jnp.where(strict, gm[0:n2, n2:2 * n2], 0.0).astype(BF16), x)
          for gm, x in zip(gram, vs_b)]
    u0 = [_dot(nb, x.astype(BF16)) for nb, x in zip(inv_b, bv)]
    a_til_b = [x.astype(BF16) for x in a_til]
    arb = [jnp.where(incl, gm[n2:2 * n2, 0:n2], 0.0).astype(BF16) for gm in gram]
    ark = [jnp.where(incl, gm[n2:2 * n2, n2:2 * n2], 0.0).astype(BF16) for gm in gram]
    r_til = [(x + _dot(m, y)).astype(BF16) for x, m, y in zip(rb, arb, a_til_b)]
    uv0 = [jnp.concatenate([x.astype(BF16), y], axis=0) for x, y in zip(u0, vs_b)]
    y0 = [_dot(jnp.concatenate([m1, m2], axis=1), x) for m1, m2, x in zip(arb, ark, uv0)]
    rhs_end = [(x * ge).astype(BF16) for x, ge in zip(rhs, g_end)]
    p_mat = [_dot_tn(x, y[0:n2]).astype(BF16) for x, y in zip(a_til_b, rhs_end)]
    q_mat = [_dot_tn(x, y) for x, y in zip(uv0, rhs_end)]

    s = [state[q] for q in range(n_pairs)]
    y_rows = []
    for ci in range(RW_STEP_CHUNKS):
        ys = []
        for q in range(n_pairs):
            un = ci * n_pairs + q
            s_b = s[q].astype(BF16)
            y_ps = _dot_nt(r_til[un], s_b) + y0[un]
            ys.append(y_ps[0:cn] + y_ps[cn:n2])
            s[q] = s[q] * g_end[un] + _dot(s_b, p_mat[un]) + q_mat[un]
        y_rows.append(jnp.concatenate(ys, axis=1))
    for q in range(n_pairs):
        state[q] = s[q]
    y = jnp.concatenate(y_rows, axis=0)

    inv_n = 1.0 / HEAD_DIM
    mean = _dot_exact_rhs(y, bd) * inv_n
    yc = y - mean
    var = _dot_exact_rhs(yc * yc, bd) * inv_n
    y = yc * lax.rsqrt(var + RW_GN_EPS) * ln_g + ln_b
    y = y + _dot_exact_rhs(r * k * r_k, bd) * v
    o_ref[0] = y * g


def _rwkv(p, mu, wl, vecs, bd):
    bsz, seq, _ = p.shape
    cn = RW_STEP_CHUNKS * RW_CHUNK
    const = lambda b, c: (0, 0)
    return pl.pallas_call(
        _rwkv_kernel,
        grid=(bsz, seq // cn),
        in_specs=[
            pl.BlockSpec((1, cn, RW_IN), lambda b, c: (b, c, OFF_RW // RW_IN)),
            pl.BlockSpec((1, RW_IN), const),
            pl.BlockSpec((RW_LORA, 3 * RW_WIDTH), const),
            pl.BlockSpec((SUBLANES, RW_WIDTH), const),
            pl.BlockSpec((RW_WIDTH, RW_WIDTH), const),
        ],
        out_specs=pl.BlockSpec((1, cn, RW_WIDTH), lambda b, c: (b, c, 0)),
        out_shape=jax.ShapeDtypeStruct((bsz, seq, RW_WIDTH), F32),
        scratch_shapes=[
            pltpu.VMEM((SUBLANES + cn, RW_IN), F32),
            pltpu.VMEM((RW_WIDTH // LANES, LANES, LANES), F32),
        ],
        compiler_params=pltpu.CompilerParams(
            dimension_semantics=("parallel", "arbitrary"), vmem_limit_bytes=VMEM_LIMIT),
        name="rwkv7",
    )(p, mu, wl, vecs, bd)


def _sb_kernel(q_ref, k_ref, v_ref, o_ref):
    tb = SB_BLOCK
    n_pairs = SB_LANES // LANES
    qi = pl.program_id(2)
    q = q_ref[0] * (HEAD_DIM ** -0.5)
    m0 = lax.broadcasted_iota(jnp.int32, (tb, LANES), 1) < HEAD_DIM
    q_stacks = [_pair_stack(q[:, pr * LANES:(pr + 1) * LANES], m0).astype(BF16)
                for pr in range(n_pairs)]
    row = lax.broadcasted_iota(jnp.int32, (2 * tb, tb), 0)
    col = lax.broadcasted_iota(jnp.int32, (2 * tb, tb), 1)
    causal = col < jnp.where(row >= tb, row - tb, row)
    krow = lax.broadcasted_iota(jnp.int32, (tb, tb), 0)
    kcol = lax.broadcasted_iota(jnp.int32, (tb, tb), 1)
    later_keys = jnp.where(krow > kcol, 1.0, 0.0).astype(BF16)

    def block(j, carry, diag):
        start = pl.multiple_of(j * tb, tb)
        out = []
        for pr in range(n_pairs):
            acc, run = carry[pr]
            lanes = slice(pr * LANES, (pr + 1) * LANES)
            k_blk = k_ref[0, pl.ds(start, tb), lanes].astype(BF16)
            v_blk = v_ref[0, pl.ds(start, tb), lanes].astype(BF16)
            z = _dot_nt(q_stacks[pr], k_blk)
            log_beta = jnp.minimum(z, 0.0) - jnp.log(1.0 + jnp.exp(-jnp.abs(z)))
            log_1m = log_beta - z
            if diag:
                log_1m = jnp.where(causal, log_1m, 0.0)
            later = _dot_exact_rhs(log_1m, later_keys) + run
            att = jnp.exp(log_beta + later)
            if diag:
                att = jnp.where(causal, att, 0.0)
            acc = acc + _dot(att.astype(BF16), v_blk)
            run = run + jnp.sum(log_1m, axis=-1, keepdims=True)
            out.append((acc, run))
        return tuple(out)

    zero = (jnp.zeros((2 * tb, LANES), F32), jnp.zeros((2 * tb, 1), F32))
    carry = block(qi, (zero,) * n_pairs, True)
    carry = lax.fori_loop(0, qi, lambda i, cr: block(qi - 1 - i, cr, False), carry)
    o_ref[0] = jnp.concatenate(
        [jnp.where(m0, acc[0:tb], acc[tb:2 * tb]) for acc, _ in carry], axis=1)


def _sb_attention(p):
    bsz, seq, _ = p.shape
    tb = SB_BLOCK
    q0 = OFF_SB // SB_LANES
    k0 = (OFF_SB + SB_WIDTH) // SB_LANES
    v0 = (OFF_SB + 2 * SB_WIDTH) // SB_LANES
    return pl.pallas_call(
        _sb_kernel,
        grid=(bsz, SB_WIDTH // SB_LANES, seq // tb),
        in_specs=[
            pl.BlockSpec((1, tb, SB_LANES), lambda b, h, i: (b, i, q0 + h)),
            pl.BlockSpec((1, seq, SB_LANES), lambda b, h, i: (b, 0, k0 + h)),
            pl.BlockSpec((1, seq, SB_LANES), lambda b, h, i: (b, 0, v0 + h)),
        ],
        out_specs=pl.BlockSpec((1, tb, SB_LANES), lambda b, h, i: (b, i, h)),
        out_shape=jax.ShapeDtypeStruct((bsz, seq, SB_WIDTH), F32),
        compiler_params=pltpu.CompilerParams(
            dimension_semantics=("parallel", "parallel", "arbitrary"),
            vmem_limit_bytes=VMEM_LIMIT),
        name="stickbreak",
    )(p, p, p)


def _mamba_kernel(xbc_ref, z_ref, dt_ref, cw_ref, cb_ref, hv_ref, dn_ref, ex_ref, o_ref,
                  xpad, state):
    ln = M2_CHUNK
    c = pl.program_id(1)

    @pl.when(c == 0)
    def _():
        xpad[0:SUBLANES, :] = jnp.zeros((SUBLANES, M2_CONV_DIM), F32)
        state[...] = jnp.zeros_like(state)

    x = xbc_ref[0]
    xpad[SUBLANES:SUBLANES + ln, :] = x
    conv = cb_ref[...] + cw_ref[M2_CONV - 1:M2_CONV, :] * x
    for i in range(M2_CONV - 1):
        sh = M2_CONV - 1 - i
        conv = conv + cw_ref[i:i + 1, :] * xpad[SUBLANES - sh:SUBLANES - sh + ln, :]
    xpad[0:SUBLANES, :] = x[ln - SUBLANES:ln, :]
    xa = _silu(conv)
    xs = xa[:, 0:M2_WIDTH]

    dt = _softplus(dt_ref[0] + hv_ref[0:1, :])
    log_a = dt * (-jnp.exp(hv_ref[1:2, :]))
    row = lax.broadcasted_iota(jnp.int32, (ln, ln), 0)
    col = lax.broadcasted_iota(jnp.int32, (ln, ln), 1)
    lower = row >= col
    ltri = jnp.where(lower, 1.0, 0.0).astype(BF16)
    acum = _dot_exact_lhs(ltri, log_a)
    acum_t = acum.T
    ex = ex_ref[...]
    dt_x = _dot_exact_rhs(dt, ex)
    acum_x = _dot_exact_rhs(acum, ex)
    xdt = xs * dt_x

    m0 = lax.broadcasted_iota(jnp.int32, (ln, LANES), 1) < HEAD_DIM
    gw = M2_GROUP_WIDTH
    heads_per_group = M2_HEADS // M2_GROUPS
    y_groups = []
    for gi in range(M2_GROUPS):
        b_g = xa[:, M2_WIDTH + gi * M2_STATE:M2_WIDTH + (gi + 1) * M2_STATE].astype(BF16)
        c_off = M2_WIDTH + M2_GROUPS * M2_STATE
        c_g = xa[:, c_off + gi * M2_STATE:c_off + (gi + 1) * M2_STATE].astype(BF16)
        cb = _dot_nt(c_g, b_g)
        gs = slice(gi * gw, (gi + 1) * gw)
        ac_g = acum_x[:, gs]
        st = state[gi]
        y_off = _dot(c_g, st.astype(BF16)) * jnp.exp(ac_g)
        pairs = []
        for q in range(heads_per_group // 2):
            mats = []
            for hh in range(2):
                h = gi * heads_per_group + 2 * q + hh
                seg = acum[:, h:h + 1] - acum_t[h:h + 1, :]
                mats.append(cb * jnp.exp(jnp.where(lower, seg, -1e30)))
            xp = xdt[:, gi * gw + q * LANES:gi * gw + (q + 1) * LANES]
            pairs.append(_dot(jnp.concatenate(mats, axis=1).astype(BF16),
                              _pair_stack(xp, m0).astype(BF16)))
        y_diag = jnp.concatenate(pairs, axis=1)
        ac_end = ac_g[ln - 1:ln, :]
        to_end = jnp.exp(ac_end - ac_g)
        state[gi] = st * jnp.exp(ac_end) + _dot_tn(b_g, (xdt[:, gs] * to_end).astype(BF16))
        y = y_diag + y_off + dn_ref[0:1, gs] * xs[:, gs]
        y = y * _silu(z_ref[0][:, gs])
        y = y * lax.rsqrt(jnp.mean(y * y, axis=-1, keepdims=True) + NORM_EPS)
        y_groups.append(y * dn_ref[1:2, gs])
    o_ref[0] = jnp.concatenate(y_groups, axis=1)


def _mamba(p, conv_w, conv_b, head_vecs, dn_vecs, expand):
    bsz, seq, _ = p.shape
    ln = M2_CHUNK
    const = lambda b, c: (0, 0)
    return pl.pallas_call(
        _mamba_kernel,
        grid=(bsz, seq // ln),
        in_specs=[
            pl.BlockSpec((1, ln, M2_CONV_DIM), lambda b, c: (b, c, OFF_XBC // M2_CONV_DIM)),
            pl.BlockSpec((1, ln, M2_WIDTH), lambda b, c: (b, c, OFF_Z // M2_WIDTH)),
            pl.BlockSpec((1, ln, LANES), lambda b, c: (b, c, OFF_DT // LANES)),
            pl.BlockSpec((M2_CONV, M2_CONV_DIM), const),
            pl.BlockSpec((1, M2_CONV_DIM), const),
            pl.BlockSpec((SUBLANES, LANES), const),
            pl.BlockSpec((SUBLANES, M2_WIDTH), const),
            pl.BlockSpec((LANES, M2_WIDTH), const),
        ],
        out_specs=pl.BlockSpec((1, ln, M2_WIDTH), lambda b, c: (b, c, 0)),
        out_shape=jax.ShapeDtypeStruct((bsz, seq, M2_WIDTH), F32),
        scratch_shapes=[
            pltpu.VMEM((SUBLANES + ln, M2_CONV_DIM), F32),
            pltpu.VMEM((M2_GROUPS, M2_STATE, M2_GROUP_WIDTH), F32),
        ],
        compiler_params=pltpu.CompilerParams(
            dimension_semantics=("parallel", "arbitrary"), vmem_limit_bytes=VMEM_LIMIT),
        name="mamba2",
    )(p, p, p, conv_w, conv_b, head_vecs, dn_vecs, expand)


def _merge_kernel(yrw_ref, ysb_ref, ym2_ref, pg_ref, x_ref, mod_ref, g2_ref,
                  wrw_ref, wsb_ref, wm2_ref, wo_ref, xo_ref, h_ref):
    d = D_MODEL
    pg = pg_ref[0]
    merged = _sigmoid(pg[:, 0:d]) * _dot(yrw_ref[0].astype(BF16), wrw_ref[...])
    merged = merged + _sigmoid(pg[:, d:2 * d]) * _dot(ysb_ref[0].astype(BF16), wsb_ref[...])
    merged = merged + _sigmoid(pg[:, 2 * d:3 * d]) * _dot(ym2_ref[0].astype(BF16), wm2_ref[...])
    xn = x_ref[0] + mod_ref[0, 2:3, :] * _dot(merged.astype(BF16), wo_ref[...])
    xo_ref[0] = xn
    h_ref[0] = _rms_mod(xn, g2_ref[...], mod_ref[0, 4:5, :], mod_ref[0, 3:4, :]).astype(BF16)


def _merge(y_rw, y_sb, y_m2, p, x, mod, g2, w_rw, w_sb, w_m2, w_o, tm):
    bsz, seq, d = x.shape
    const = lambda b, i: (0, 0)
    tok = lambda w: pl.BlockSpec((1, tm, w), lambda b, i: (b, i, 0))
    return pl.pallas_call(
        _merge_kernel,
        grid=(bsz, seq // tm),
        in_specs=[
            tok(RW_WIDTH), tok(SB_WIDTH), tok(M2_WIDTH),
            pl.BlockSpec((1, tm, 3 * d), lambda b, i: (b, i, OFF_GATE // (3 * d))),
            tok(d),
            pl.BlockSpec((1, N_MOD, d), lambda b, i: (b, 0, 0)),
            pl.BlockSpec((1, d), const),
            pl.BlockSpec((RW_WIDTH, d), const),
            pl.BlockSpec((SB_WIDTH, d), const),
            pl.BlockSpec((M2_WIDTH, d), const),
            pl.BlockSpec((d, d), const),
        ],
        out_specs=[tok(d), tok(d)],
        out_shape=[jax.ShapeDtypeStruct((bsz, seq, d), F32),
                   jax.ShapeDtypeStruct((bsz, seq, d), BF16)],
        compiler_params=pltpu.CompilerParams(
            dimension_semantics=("parallel", "parallel"), vmem_limit_bytes=VMEM_LIMIT),
        name="merge",
    )(y_rw, y_sb, y_m2, p, x, mod, g2.reshape(1, d), w_rw, w_sb, w_m2, w_o)


def _ffn_kernel(h_ref, x_ref, mod_ref, wu_ref, cw_ref, cb_ref, wd_ref, fg_ref, o_ref,
                ubuf, carry, *, final):
    tm = h_ref.shape[1]
    fc = FFN_COL_CHUNK

    @pl.when(pl.program_id(1) == 0)
    def _():
        carry[...] = jnp.zeros_like(carry)

    h = h_ref[0]
    acc = jnp.zeros((tm, D_MODEL), F32)
    for ci in range(D_FF // fc):
        halves = []
        for half in range(2):
            cols = slice(half * D_FF + ci * fc, half * D_FF + (ci + 1) * fc)
            u = _dot(h, wu_ref[:, cols])
            ubuf[0:SUBLANES, :] = carry[:, cols]
            ubuf[SUBLANES:SUBLANES + tm, :] = u
            carry[:, cols] = u[tm - SUBLANES:tm, :]
            conv = cb_ref[:, cols] + cw_ref[FFN_CONV - 1:FFN_CONV, cols] * u
            for i in range(FFN_CONV - 1):
                sh = FFN_CONV - 1 - i
                conv = conv + cw_ref[i:i + 1, cols] * ubuf[SUBLANES - sh:SUBLANES - sh + tm, :]
            halves.append(conv)
        act = _silu(halves[0]) * halves[1]
        acc = acc + _dot(act.astype(BF16), wd_ref[ci * fc:(ci + 1) * fc, :])
    xn = x_ref[0] + mod_ref[0, 5:6, :] * acc
    if final:
        xn = xn * lax.rsqrt(jnp.mean(xn * xn, axis=-1, keepdims=True) + NORM_EPS) * fg_ref[...]
    o_ref[0] = xn


def _ffn(h2, x, mod, w_up, conv_w, conv_b, w_down, final_g, tm, final):
    bsz, seq, d = x.shape
    const = lambda b, i: (0, 0)
    tok = pl.BlockSpec((1, tm, d), lambda b, i: (b, i, 0))
    resident = lambda shape: pl.BlockSpec(shape, const, pipeline_mode=pl.Buffered(1))
    return pl.pallas_call(
        functools.partial(_ffn_kernel, final=final),
        grid=(bsz, seq // tm),
        in_specs=[
            tok, tok,
            pl.BlockSpec((1, N_MOD, d), lambda b, i: (b, 0, 0)),
            resident((d, 2 * D_FF)),
            pl.BlockSpec((FFN_CONV, 2 * D_FF), const),
            pl.BlockSpec((1, 2 * D_FF), const),
            resident((D_FF, d)),
            pl.BlockSpec((1, d), const),
        ],
        out_specs=tok,
        out_shape=jax.ShapeDtypeStruct((bsz, seq, d), F32),
        scratch_shapes=[
            pltpu.VMEM((SUBLANES + tm, FFN_COL_CHUNK), F32),
            pltpu.VMEM((SUBLANES, 2 * D_FF), F32),
        ],
        compiler_params=pltpu.CompilerParams(
            dimension_semantics=("parallel", "arbitrary"), vmem_limit_bytes=VMEM_LIMIT),
        name="convffn",
    )(h2, x, mod, w_up, conv_w, conv_b.reshape(1, -1), w_down, final_g.reshape(1, d))


def _in_weight(w_in):
    rw, sb = RW_IN, 3 * SB_WIDTH
    z0 = rw + sb
    x0 = z0 + M2_WIDTH
    t0 = x0 + M2_CONV_DIM
    g0 = t0 + M2_HEADS
    d = w_in.shape[0]
    parts = [w_in[:, x0:t0], w_in[:, rw:z0], w_in[:, g0:g0 + 3 * D_MODEL], w_in[:, z0:x0],
             w_in[:, 0:rw], w_in[:, t0:g0],
             jnp.zeros((d, P_WIDTH - OFF_DT - M2_HEADS), w_in.dtype)]
    return jnp.concatenate(parts, axis=1).astype(BF16)


def _rwkv_lora_weight(w2, a2, g2):
    wl = jnp.zeros((RW_LORA, 3 * RW_WIDTH), F32)
    wl = wl.at[0:64, 0:RW_WIDTH].set(w2)
    wl = wl.at[64:128, RW_WIDTH:2 * RW_WIDTH].set(a2)
    wl = wl.at[128:256, 2 * RW_WIDTH:3 * RW_WIDTH].set(g2)
    return wl.astype(BF16)


def _rows(vectors, width, rows=SUBLANES):
    out = jnp.zeros((rows, width), F32)
    for i, vec in enumerate(vectors):
        out = out.at[i, 0:vec.shape[0]].set(vec.astype(F32))
    return out


def kernel(x, c, ada_w, ada_b, norm1_g, norm2_g, w_in, rw_mu, rw_w0, rw_w2, rw_a0, rw_a2,
           rw_g2, rw_k_k, rw_k_a, rw_r_k, rw_ln_g, rw_ln_b, rw_wo, sb_wo, m2_conv_w,
           m2_conv_b, m2_dt_bias, m2_a_log, m2_d, m2_norm_g, m2_wo, w_out, ffn_w_up,
           ffn_conv_w, ffn_conv_b, ffn_w_down, final_norm_g):
    bsz, seq, d = x.shape
    depth = ada_w.shape[0]
    assert d == D_MODEL and seq % SB_BLOCK == 0
    tm_in = min(1024, seq)
    tm_merge = min(512, seq)
    tm_ffn = min(256, seq)

    head = jnp.arange(RW_WIDTH) // HEAD_DIM
    same_head = (head[:, None] == head[None, :]).astype(BF16)
    lane_head = jnp.arange(M2_WIDTH) // HEAD_DIM
    expand = (jnp.arange(LANES)[:, None] == lane_head[None, :]).astype(BF16)

    mod_all = _modulation(c, ada_w, ada_b)
    for l in range(depth):
        mod = mod_all[l].reshape(bsz, N_MOD, d)
        p = _inproj(x, mod, norm1_g[l], _in_weight(w_in[l]), tm_in)
        rw_vecs = _rows([rw_w0[l], rw_a0[l], rw_k_k[l], rw_k_a[l], rw_r_k[l].reshape(-1),
                         rw_ln_g[l], rw_ln_b[l]], RW_WIDTH)
        y_rw = _rwkv(p, rw_mu[l].reshape(1, RW_IN),
                     _rwkv_lora_weight(rw_w2[l], rw_a2[l], rw_g2[l]), rw_vecs, same_head)
        y_sb = _sb_attention(p)
        head_vecs = _rows([m2_dt_bias[l], m2_a_log[l]], LANES)
        dn_vecs = _rows([jnp.repeat(m2_d[l], HEAD_DIM), m2_norm_g[l]], M2_WIDTH)
        y_m2 = _mamba(p, m2_conv_w[l], m2_conv_b[l].reshape(1, -1), head_vecs, dn_vecs, expand)
        x, h2 = _merge(y_rw, y_sb, y_m2, p, x, mod, norm2_g[l], rw_wo[l].astype(BF16),
                       sb_wo[l].astype(BF16), m2_wo[l].astype(BF16), w_out[l].astype(BF16),
                       tm_merge)
        x = _ffn(h2, x, mod, ffn_w_up[l].astype(BF16), ffn_conv_w[l], ffn_conv_b[l],
                 ffn_w_down[l].astype(BF16), final_norm_g, tm_ffn, final=(l == depth - 1))
    return x
```

```python
import functools

import jax
import jax.numpy as jnp
from jax import lax
from jax.experimental import pallas as pl
from jax.experimental.pallas import tpu as pltpu

F32 = jnp.float32
BF16 = jnp.bfloat16

D_MODEL = 1024
NORM_EPS = 1e-6
N_MOD = 6

RW_HEADS = 8
HEAD_DIM = 64
RW_WIDTH = RW_HEADS * HEAD_DIM
RW_LORA = 256
RW_IN = 3 * RW_WIDTH + RW_LORA
RW_GN_EPS = 64e-5
RW_CHUNK = 64
RW_STEP_CHUNKS = 4
RW_UNIT_LANES = 256

SB_WIDTH = 512
SB_BLOCK = 256
SB_LANES = 256
SB_MASKED = -1e30

M2_HEADS = 16
M2_WIDTH = M2_HEADS * HEAD_DIM
M2_STATE = 128
M2_GROUPS = 2
M2_GROUP_WIDTH = M2_WIDTH // M2_GROUPS
M2_CONV = 4
M2_CHUNK = 128
M2_CONV_DIM = M2_WIDTH + 2 * M2_GROUPS * M2_STATE

D_FF = 2816
FFN_CONV = 3
FFN_COL_CHUNK = 1408

LANES = 128
SUBLANES = 8

OFF_XBC = 0
OFF_SB = OFF_XBC + M2_CONV_DIM
OFF_GATE = OFF_SB + 3 * SB_WIDTH
OFF_Z = OFF_GATE + 3 * D_MODEL
OFF_RW = OFF_Z + M2_WIDTH
OFF_DT = OFF_RW + RW_IN
P_WIDTH = 9216
IN_COL_TILE = 1024

VMEM_LIMIT = 56 * 1024 * 1024


def _dot(a, b):
    return jnp.dot(a, b, preferred_element_type=F32)


def _dot_nt(a, b):
    return lax.dot_general(a, b, (((1,), (1,)), ((), ())), preferred_element_type=F32)


def _dot_tn(a, b):
    return lax.dot_general(a, b, (((0,), (0,)), ((), ())), preferred_element_type=F32)


def _split(x):
    hi = x.astype(BF16)
    lo = (x - hi.astype(F32)).astype(BF16)
    return hi, lo


def _dot_exact_rhs(x, m):
    hi, lo = _split(x)
    return _dot(hi, m) + _dot(lo, m)


def _dot_exact_lhs(m, x):
    hi, lo = _split(x)
    return _dot(m, hi) + _dot(m, lo)


def _sigmoid(x):
    return 1.0 / (1.0 + jnp.exp(-x))


def _silu(x):
    return x * _sigmoid(x)


def _softplus(x):
    return jnp.maximum(x, 0.0) + jnp.log(1.0 + jnp.exp(-jnp.abs(x)))


def _mod_kernel(c_ref, w_ref, b_ref, o_ref):
    c_act = _silu(c_ref[...]).astype(BF16)
    o_ref[0] = _dot(c_act, w_ref[0].astype(BF16)) + b_ref[0]


def _modulation(c, ada_w, ada_b):
    depth, d, n = ada_w.shape
    bsz = c.shape[0]
    tile = 1024
    return pl.pallas_call(
        _mod_kernel,
        grid=(depth, n // tile),
        in_specs=[
            pl.BlockSpec((bsz, d), lambda l, j: (0, 0)),
            pl.BlockSpec((1, d, tile), lambda l, j: (l, 0, j)),
            pl.BlockSpec((1, 1, tile), lambda l, j: (l, 0, j)),
        ],
        out_specs=pl.BlockSpec((1, bsz, tile), lambda l, j: (l, 0, j)),
        out_shape=jax.ShapeDtypeStruct((depth, bsz, n), F32),
        compiler_params=pltpu.CompilerParams(
            dimension_semantics=("parallel", "parallel"), vmem_limit_bytes=VMEM_LIMIT),
        name="ada_mod",
    )(c, ada_w, ada_b.reshape(depth, 1, n))


def _rms_mod(x, g, scale, shift):
    y = x * lax.rsqrt(jnp.mean(x * x, axis=-1, keepdims=True) + NORM_EPS) * g
    return y * (1.0 + scale) + shift


def _inproj_kernel(x_ref, mod_ref, g_ref, w_ref, o_ref, h_scr):
    @pl.when(pl.program_id(2) == 0)
    def _():
        h = _rms_mod(x_ref[0], g_ref[...], mod_ref[0, 1:2, :], mod_ref[0, 0:1, :])
        h_scr[...] = h.astype(BF16)

    o_ref[0] = _dot(h_scr[...], w_ref[...])


def _inproj(x, mod, g, wcat, tm):
    bsz, seq, d = x.shape
    return pl.pallas_call(
        _inproj_kernel,
        grid=(bsz, seq // tm, P_WIDTH // IN_COL_TILE),
        in_specs=[
            pl.BlockSpec((1, tm, d), lambda b, i, j: (b, i, 0)),
            pl.BlockSpec((1, N_MOD, d), lambda b, i, j: (b, 0, 0)),
            pl.BlockSpec((1, d), lambda b, i, j: (0, 0)),
            pl.BlockSpec((d, IN_COL_TILE), lambda b, i, j: (0, j)),
        ],
        out_specs=pl.BlockSpec((1, tm, IN_COL_TILE), lambda b, i, j: (b, i, j)),
        out_shape=jax.ShapeDtypeStruct((bsz, seq, P_WIDTH), F32),
        scratch_shapes=[pltpu.VMEM((tm, d), BF16)],
        compiler_params=pltpu.CompilerParams(
            dimension_semantics=("parallel", "parallel", "arbitrary"),
            vmem_limit_bytes=VMEM_LIMIT),
        name="inproj",
    )(x, mod, g.reshape(1, d), wcat)


def _pair_stack(x, m0):
    return jnp.concatenate([jnp.where(m0, x, 0.0), jnp.where(m0, 0.0, x)], axis=0)


def _side_by_side(x, left):
    return jnp.concatenate([jnp.where(left, x, 0), jnp.where(left, 0, x)], axis=0)


def _unit_lower_inverse(mats, block, left):
    n = mats[0].shape[0]
    row = lax.broadcasted_iota(jnp.int32, (n, 2 * n), 0)
    col = lax.broadcasted_iota(jnp.int32, (n, 2 * n), 1) & (n - 1)

    def quadrant(l):
        same = (row >> l) == (col >> l)
        low = ((row >> (l - 1)) & 1) == 1
        first = ((col >> (l - 1)) & 1) == 0
        return same & low & first

    eye = jnp.where(row == col, 1.0, 0.0)
    first_level = quadrant(1)
    inv = [eye + jnp.where(first_level, a, 0.0) for a in mats]
    level = 2
    while (1 << level) <= block:
        quad = quadrant(level)
        inv_b = [x.astype(BF16) for x in inv]
        t = [_dot(jnp.where(quad, a, 0.0).astype(BF16), _side_by_side(xb, left))
             for a, xb in zip(mats, inv_b)]
        inv = [x + _dot(xb, _side_by_side(tt.astype(BF16), left))
               for x, xb, tt in zip(inv, inv_b, t)]
        level += 1
    return inv


def _rwkv_kernel(p_ref, mu_ref, wl_ref, vec_ref, bd_ref, o_ref, pbuf, state):
    cn = RW_CHUNK
    tb = RW_STEP_CHUNKS * cn
    n_units = RW_WIDTH // RW_UNIT_LANES
    c = pl.program_id(1)

    @pl.when(c == 0)
    def _():
        pbuf[0:SUBLANES, :] = jnp.zeros((SUBLANES, RW_IN), F32)
        state[...] = jnp.zeros_like(state)

    p = p_ref[0]
    pbuf[SUBLANES:SUBLANES + tb, :] = p
    prev = pbuf[SUBLANES - 1:SUBLANES - 1 + tb, :]
    pbuf[0:SUBLANES, :] = p[tb - SUBLANES:tb, :]
    pm = p + (prev - p) * mu_ref[...]

    w0 = vec_ref[0:1, :]
    a0 = vec_ref[1:2, :]
    k_k = vec_ref[2:3, :]
    k_a = vec_ref[3:4, :]
    r_k = vec_ref[4:5, :]
    ln_g = vec_ref[5:6, :]
    ln_b = vec_ref[6:7, :]
    bd = bd_ref[...]

    def head_sum(x):
        wl = RW_UNIT_LANES
        return jnp.concatenate([_dot_exact_rhs(x[:, w * wl:(w + 1) * wl], bd)
                                for w in range(n_units)], axis=1)

    r = pm[:, 0:RW_WIDTH]
    k = pm[:, RW_WIDTH:2 * RW_WIDTH]
    v = pm[:, 2 * RW_WIDTH:3 * RW_WIDTH]
    lo = pm[:, 3 * RW_WIDTH:RW_IN]
    lane = lax.broadcasted_iota(jnp.int32, lo.shape, 1)
    act = jnp.where(lane < 64, jnp.tanh(lo), jnp.where(lane < 128, lo, _sigmoid(lo)))
    proj = _dot(act.astype(BF16), wl_ref[...])

    log_w = -_softplus(-(w0 + proj[:, 0:RW_WIDTH])) - 0.5
    ld = -jnp.exp(log_w)
    a = _sigmoid(a0 + proj[:, RW_WIDTH:2 * RW_WIDTH])
    g = proj[:, 2 * RW_WIDTH:3 * RW_WIDTH]

    kk = k * k_k
    kk = kk * lax.rsqrt(jnp.maximum(head_sum(kk * kk), 1e-24))
    k = k * (1.0 + (a - 1.0) * k_a)

    trow = lax.broadcasted_iota(jnp.int32, (tb, tb), 0)
    tcol = lax.broadcasted_iota(jnp.int32, (tb, tb), 1)
    shift = cn.bit_length() - 1
    same_chunk = (trow >> shift) == (tcol >> shift)
    ltri = jnp.where((trow >= tcol) & same_chunk, 1.0, 0.0).astype(BF16)
    cum = _dot_exact_lhs(ltri, ld)
    g_incl = jnp.exp(cum)
    g_inv = jnp.exp(-cum)
    al_bar = -kk * jnp.exp(cum - ld)
    r_bar = r * g_incl
    be_t = kk * a * g_inv
    k_t = k * g_inv

    wl = RW_UNIT_LANES
    n2 = 2 * cn
    lane_w = lax.broadcasted_iota(jnp.int32, (cn, wl), 1)
    first_head = (lane_w & HEAD_DIM) == 0
    left = lax.broadcasted_iota(jnp.int32, (n2, wl), 1) < LANES
    srow = lax.broadcasted_iota(jnp.int32, (n2, n2), 0)
    scol = lax.broadcasted_iota(jnp.int32, (n2, n2), 1)
    strict = srow > scol
    incl = srow >= scol
    krow = lax.broadcasted_iota(jnp.int32, (wl, wl), 0)
    kcol = lax.broadcasted_iota(jnp.int32, (wl, wl), 1)
    same_pair = (krow < LANES) == (kcol < LANES)

    units = [(ci, w) for ci in range(RW_STEP_CHUNKS) for w in range(n_units)]

    def stacked(x, ci, w):
        return _pair_stack(x[ci * cn:(ci + 1) * cn, w * wl:(w + 1) * wl], first_head)

    al_b = [stacked(al_bar, ci, w).astype(BF16) for ci, w in units]
    rb = [stacked(r_bar, ci, w) for ci, w in units]
    vs_b = [stacked(v, ci, w).astype(BF16) for ci, w in units]
    bt = [stacked(be_t, ci, w) for ci, w in units]
    kt = [stacked(k_t, ci, w) for ci, w in units]
    g_end = [jnp.exp(cum[(ci + 1) * cn - 1:(ci + 1) * cn, w * wl:(w + 1) * wl])
             for ci, w in units]

    lhs = [jnp.concatenate([x, y.astype(BF16)], axis=0) for x, y in zip(al_b, rb)]
    rhs = [jnp.concatenate([x, y], axis=0).astype(BF16) for x, y in zip(bt, kt)]
    gram = [(_dot_nt(x[:, 0:LANES], y[:, 0:LANES]), _dot_nt(x[:, LANES:wl], y[:, LANES:wl]))
            for x, y in zip(lhs, rhs)]

    def blocks(gm, rows, cols, mask):
        return jnp.concatenate([jnp.where(mask, m[rows, cols], 0.0) for m in gm], axis=1)

    top, bottom = slice(0, n2), slice(n2, 2 * n2)
    a_mat = [blocks(gm, top, top, strict) for gm in gram]
    b_mat = [blocks(gm, top, bottom, strict).astype(BF16) for gm in gram]
    arb = [blocks(gm, bottom, top, incl).astype(BF16) for gm in gram]
    ark = [blocks(gm, bottom, bottom, incl).astype(BF16) for gm in gram]
    inv_b = [x.astype(BF16) for x in _unit_lower_inverse(a_mat, cn, left)]
    vs_sbs = [_side_by_side(x, left) for x in vs_b]
    a_til = [_dot(nb, _side_by_side(x, left)).astype(BF16) for nb, x in zip(inv_b, al_b)]
    bv = [_dot(m, x) for m, x in zip(b_mat, vs_sbs)]
    u0 = [_dot(nb, _side_by_side(x.astype(BF16), left)).astype(BF16)
          for nb, x in zip(inv_b, bv)]
    r_til = [(x + _dot(m, _side_by_side(y, left))).astype(BF16)
             for x, m, y in zip(rb, arb, a_til)]
    y0 = [_dot(m1, _side_by_side(x, left)) + _dot(m2, y)
          for m1, x, m2, y in zip(arb, u0, ark, vs_sbs)]
    bt_end = [(x * ge).astype(BF16) for x, ge in zip(bt, g_end)]
    kt_end = [(x * ge).astype(BF16) for x, ge in zip(kt, g_end)]
    p_mat = [jnp.where(same_pair, _dot_tn(x, y), 0.0).astype(BF16) for x, y in zip(a_til, bt_end)]
    q_full = [_dot_tn(jnp.concatenate([x, y], axis=0), jnp.concatenate([z, t], axis=0))
              for x, y, z, t in zip(u0, vs_b, bt_end, kt_end)]
    q_mat = [jnp.concatenate([m[0:LANES, 0:LANES], m[LANES:wl, LANES:wl]], axis=1)
             for m in q_full]

    s = [state[w] for w in range(n_units)]
    y_rows = []
    for ci in range(RW_STEP_CHUNKS):
        ys = []
        for w in range(n_units):
            un = ci * n_units + w
            s_b = s[w].astype(BF16)
            y_ps = _dot_nt(r_til[un], _side_by_side(s_b, left)) + y0[un]
            ys.append(y_ps[0:cn] + y_ps[cn:n2])
            s[w] = s[w] * g_end[un] + _dot(s_b, p_mat[un]) + q_mat[un]
        y_rows.append(jnp.concatenate(ys, axis=1))
    for w in range(n_units):
        state[w] = s[w]
    y = jnp.concatenate(y_rows, axis=0)

    inv_n = 1.0 / HEAD_DIM
    mean = head_sum(y) * inv_n
    yc = y - mean
    var = head_sum(yc * yc) * inv_n
    y = yc * lax.rsqrt(var + RW_GN_EPS) * ln_g + ln_b
    y = y + head_sum(r * k * r_k) * v
    o_ref[0] = y * g


def _rwkv(p, mu, wl, vecs, bd):
    bsz, seq, _ = p.shape
    cn = RW_STEP_CHUNKS * RW_CHUNK
    const = lambda b, c: (0, 0)
    return pl.pallas_call(
        _rwkv_kernel,
        grid=(bsz, seq // cn),
        in_specs=[
            pl.BlockSpec((1, cn, RW_IN), lambda b, c: (b, c, OFF_RW // RW_IN)),
            pl.BlockSpec((1, RW_IN), const),
            pl.BlockSpec((RW_LORA, 3 * RW_WIDTH), const),
            pl.BlockSpec((SUBLANES, RW_WIDTH), const),
            pl.BlockSpec((RW_UNIT_LANES, RW_UNIT_LANES), const),
        ],
        out_specs=pl.BlockSpec((1, cn, RW_WIDTH), lambda b, c: (b, c, 0)),
        out_shape=jax.ShapeDtypeStruct((bsz, seq, RW_WIDTH), F32),
        scratch_shapes=[
            pltpu.VMEM((SUBLANES + cn, RW_IN), F32),
            pltpu.VMEM((RW_WIDTH // RW_UNIT_LANES, LANES, RW_UNIT_LANES), F32),
        ],
        compiler_params=pltpu.CompilerParams(
            dimension_semantics=("parallel", "arbitrary"), vmem_limit_bytes=VMEM_LIMIT),
        name="rwkv7",
    )(p, mu, wl, vecs, bd)


def _sb_kernel(q_ref, k_ref, v_ref, o_ref, lb_scr, l1m_scr):
    tb = SB_BLOCK
    n_pairs = SB_LANES // LANES
    qi = pl.program_id(2)
    q = q_ref[0] * (HEAD_DIM ** -0.5)
    m0 = lax.broadcasted_iota(jnp.int32, (tb, LANES), 1) < HEAD_DIM
    q_stacks = [_pair_stack(q[:, pr * LANES:(pr + 1) * LANES], m0).astype(BF16)
                for pr in range(n_pairs)]
    row = lax.broadcasted_iota(jnp.int32, (2 * tb, tb), 0)
    col = lax.broadcasted_iota(jnp.int32, (2 * tb, tb), 1)
    causal = col < jnp.where(row >= tb, row - tb, row)
    krow = lax.broadcasted_iota(jnp.int32, (tb, tb), 0)
    kcol = lax.broadcasted_iota(jnp.int32, (tb, tb), 1)
    later_keys = jnp.where(krow > kcol, 1.0, 0.0).astype(BF16)

    def scores(j, slot, diag):
        start = pl.multiple_of(j * tb, tb)
        sums = []
        for pr in range(n_pairs):
            k_blk = k_ref[0, pl.ds(start, tb), pr * LANES:(pr + 1) * LANES].astype(BF16)
            z = _dot_nt(q_stacks[pr], k_blk)
            log_beta = jnp.minimum(z, 0.0) - jnp.log(1.0 + jnp.exp(-jnp.abs(z)))
            log_1m = log_beta - z
            if diag:
                log_1m = jnp.where(causal, log_1m, 0.0)
                log_beta = jnp.where(causal, log_beta, SB_MASKED)
            lb_scr[slot, pr] = log_beta
            l1m_scr[slot, pr] = log_1m.astype(BF16)
            sums.append(jnp.sum(log_1m, axis=-1, keepdims=True))
        return tuple(sums)

    def weights(j, slot, run, acc):
        start = pl.multiple_of(j * tb, tb)
        out = []
        for pr in range(n_pairs):
            v_blk = v_ref[0, pl.ds(start, tb), pr * LANES:(pr + 1) * LANES].astype(BF16)
            later = _dot(l1m_scr[slot, pr], later_keys) + run[pr]
            att = jnp.exp(lb_scr[slot, pr] + later)
            out.append(acc[pr] + _dot(att.astype(BF16), v_blk))
        return tuple(out)

    def step(i, carry):
        acc, run, sums = carry
        slot = i & 1
        new_sums = scores(qi - 1 - i, 1 - slot, False)
        acc = weights(qi - i, slot, run, acc)
        run = tuple(r + s for r, s in zip(run, sums))
        return acc, run, new_sums

    acc0 = tuple(jnp.zeros((2 * tb, LANES), F32) for _ in range(n_pairs))
    run0 = tuple(jnp.zeros((2 * tb, 1), F32) for _ in range(n_pairs))
    acc, run, _ = lax.fori_loop(0, qi, step, (acc0, run0, scores(qi, 0, True)))
    acc = weights(0, qi & 1, run, acc)
    o_ref[0] = jnp.concatenate(
        [jnp.where(m0, a[0:tb], a[tb:2 * tb]) for a in acc], axis=1)


def _sb_attention(p):
    bsz, seq, _ = p.shape
    tb = SB_BLOCK
    q0 = OFF_SB // SB_LANES
    k0 = (OFF_SB + SB_WIDTH) // SB_LANES
    v0 = (OFF_SB + 2 * SB_WIDTH) // SB_LANES
    return pl.pallas_call(
        _sb_kernel,
        grid=(bsz, SB_WIDTH // SB_LANES, seq // tb),
        in_specs=[
            pl.BlockSpec((1, tb, SB_LANES), lambda b, h, i: (b, i, q0 + h)),
            pl.BlockSpec((1, seq, SB_LANES), lambda b, h, i: (b, 0, k0 + h)),
            pl.BlockSpec((1, seq, SB_LANES), lambda b, h, i: (b, 0, v0 + h)),
        ],
        out_specs=pl.BlockSpec((1, tb, SB_LANES), lambda b, h, i: (b, i, h)),
        out_shape=jax.ShapeDtypeStruct((bsz, seq, SB_WIDTH), F32),
        scratch_shapes=[
            pltpu.VMEM((2, SB_LANES // LANES, 2 * tb, tb), F32),
            pltpu.VMEM((2, SB_LANES // LANES, 2 * tb, tb), BF16),
        ],
        compiler_params=pltpu.CompilerParams(
            dimension_semantics=("parallel", "parallel", "arbitrary"),
            vmem_limit_bytes=VMEM_LIMIT),
        name="stickbreak",
    )(p, p, p)


def _mamba_kernel(xbc_ref, z_ref, dt_ref, cw_ref, cb_ref, hv_ref, dn_ref, ex_ref, o_ref,
                  xpad, state):
    ln = M2_CHUNK
    c = pl.program_id(1)

    @pl.when(c == 0)
    def _():
        xpad[0:SUBLANES, :] = jnp.zeros((SUBLANES, M2_CONV_DIM), F32)
        state[...] = jnp.zeros_like(state)

    x = xbc_ref[0]
    xpad[SUBLANES:SUBLANES + ln, :] = x
    conv = cb_ref[...] + cw_ref[M2_CONV - 1:M2_CONV, :] * x
    for i in range(M2_CONV - 1):
        sh = M2_CONV - 1 - i
        conv = conv + cw_ref[i:i + 1, :] * xpad[SUBLANES - sh:SUBLANES - sh + ln, :]
    xpad[0:SUBLANES, :] = x[ln - SUBLANES:ln, :]
    xa = _silu(conv)
    xs = xa[:, 0:M2_WIDTH]

    dt = _softplus(dt_ref[0] + hv_ref[0:1, :])
    log_a = dt * (-jnp.exp(hv_ref[1:2, :]))
    row = lax.broadcasted_iota(jnp.int32, (ln, ln), 0)
    col = lax.broadcasted_iota(jnp.int32, (ln, ln), 1)
    lower = row >= col
    ltri = jnp.where(lower, 1.0, 0.0).astype(BF16)
    acum = _dot_exact_lhs(ltri, log_a)
    acum_t = acum.T
    ex = ex_ref[...]
    dt_x = _dot_exact_rhs(dt, ex)
    acum_x = _dot_exact_rhs(acum, ex)
    xdt = xs * dt_x

    m0 = lax.broadcasted_iota(jnp.int32, (ln, LANES), 1) < HEAD_DIM
    gw = M2_GROUP_WIDTH
    heads_per_group = M2_HEADS // M2_GROUPS
    y_groups = []
    for gi in range(M2_GROUPS):
        b_g = xa[:, M2_WIDTH + gi * M2_STATE:M2_WIDTH + (gi + 1) * M2_STATE].astype(BF16)
        c_off = M2_WIDTH + M2_GROUPS * M2_STATE
        c_g = xa[:, c_off + gi * M2_STATE:c_off + (gi + 1) * M2_STATE].astype(BF16)
        cb = _dot_nt(c_g, b_g)
        gs = slice(gi * gw, (gi + 1) * gw)
        ac_g = acum_x[:, gs]
        st = state[gi]
        y_off = _dot(c_g, st.astype(BF16)) * jnp.exp(ac_g)
        pairs = []
        for q in range(heads_per_group // 2):
            mats = []
            for hh in range(2):
                h = gi * heads_per_group + 2 * q + hh
                seg = acum[:, h:h + 1] - acum_t[h:h + 1, :]
                mats.append(cb * jnp.exp(jnp.where(lower, seg, -1e30)))
            xp = xdt[:, gi * gw + q * LANES:gi * gw + (q + 1) * LANES]
            pairs.append(_dot(jnp.concatenate(mats, axis=1).astype(BF16),
                              _pair_stack(xp, m0).astype(BF16)))
        y_diag = jnp.concatenate(pairs, axis=1)
        ac_end = ac_g[ln - 1:ln, :]
        to_end = jnp.exp(ac_end - ac_g)
        state[gi] = st * jnp.exp(ac_end) + _dot_tn(b_g, (xdt[:, gs] * to_end).astype(BF16))
        y = y_diag + y_off + dn_ref[0:1, gs] * xs[:, gs]
        y = y * _silu(z_ref[0][:, gs])
        y = y * lax.rsqrt(jnp.mean(y * y, axis=-1, keepdims=True) + NORM_EPS)
        y_groups.append(y * dn_ref[1:2, gs])
    o_ref[0] = jnp.concatenate(y_groups, axis=1)


def _mamba(p, conv_w, conv_b, head_vecs, dn_vecs, expand):
    bsz, seq, _ = p.shape
    ln = M2_CHUNK
    const = lambda b, c: (0, 0)
    return pl.pallas_call(
        _mamba_kernel,
        grid=(bsz, seq // ln),
        in_specs=[
            pl.BlockSpec((1, ln, M2_CONV_DIM), lambda b, c: (b, c, OFF_XBC // M2_CONV_DIM)),
            pl.BlockSpec((1, ln, M2_WIDTH), lambda b, c: (b, c, OFF_Z // M2_WIDTH)),
            pl.BlockSpec((1, ln, LANES), lambda b, c: (b, c, OFF_DT // LANES)),
            pl.BlockSpec((M2_CONV, M2_CONV_DIM), const),
            pl.BlockSpec((1, M2_CONV_DIM), const),
            pl.BlockSpec((SUBLANES, LANES), const),
            pl.BlockSpec((SUBLANES, M2_WIDTH), const),
            pl.BlockSpec((LANES, M2_WIDTH), const),
        ],
        out_specs=pl.BlockSpec((1, ln, M2_WIDTH), lambda b, c: (b, c, 0)),
        out_shape=jax.ShapeDtypeStruct((bsz, seq, M2_WIDTH), F32),
        scratch_shapes=[
            pltpu.VMEM((SUBLANES + ln, M2_CONV_DIM), F32),
            pltpu.VMEM((M2_GROUPS, M2_STATE, M2_GROUP_WIDTH), F32),
        ],
        compiler_params=pltpu.CompilerParams(
            dimension_semantics=("parallel", "arbitrary"), vmem_limit_bytes=VMEM_LIMIT),
        name="mamba2",
    )(p, p, p, conv_w, conv_b, head_vecs, dn_vecs, expand)


def _merge_kernel(yrw_ref, ysb_ref, ym2_ref, pg_ref, x_ref, mod_ref, g2_ref,
                  wrw_ref, wsb_ref, wm2_ref, wo_ref, xo_ref, h_ref):
    d = D_MODEL
    pg = pg_ref[0]
    merged = _sigmoid(pg[:, 0:d]) * _dot(yrw_ref[0].astype(BF16), wrw_ref[...])
    merged = merged + _sigmoid(pg[:, d:2 * d]) * _dot(ysb_ref[0].astype(BF16), wsb_ref[...])
    merged = merged + _sigmoid(pg[:, 2 * d:3 * d]) * _dot(ym2_ref[0].astype(BF16), wm2_ref[...])
    xn = x_ref[0] + mod_ref[0, 2:3, :] * _dot(merged.astype(BF16), wo_ref[...])
    xo_ref[0] = xn
    h_ref[0] = _rms_mod(xn, g2_ref[...], mod_ref[0, 4:5, :], mod_ref[0, 3:4, :]).astype(BF16)


def _merge(y_rw, y_sb, y_m2, p, x, mod, g2, w_rw, w_sb, w_m2, w_o, tm):
    bsz, seq, d = x.shape
    const = lambda b, i: (0, 0)
    tok = lambda w: pl.BlockSpec((1, tm, w), lambda b, i: (b, i, 0))
    return pl.pallas_call(
        _merge_kernel,
        grid=(bsz, seq // tm),
        in_specs=[
            tok(RW_WIDTH), tok(SB_WIDTH), tok(M2_WIDTH),
            pl.BlockSpec((1, tm, 3 * d), lambda b, i: (b, i, OFF_GATE // (3 * d))),
            tok(d),
            pl.BlockSpec((1, N_MOD, d), lambda b, i: (b, 0, 0)),
            pl.BlockSpec((1, d), const),
            pl.BlockSpec((RW_WIDTH, d), const),
            pl.BlockSpec((SB_WIDTH, d), const),
            pl.BlockSpec((M2_WIDTH, d), const),
            pl.BlockSpec((d, d), const),
        ],
        out_specs=[tok(d), tok(d)],
        out_shape=[jax.ShapeDtypeStruct((bsz, seq, d), F32),
                   jax.ShapeDtypeStruct((bsz, seq, d), BF16)],
        compiler_params=pltpu.CompilerParams(
            dimension_semantics=("parallel", "parallel"), vmem_limit_bytes=VMEM_LIMIT),
        name="merge",
    )(y_rw, y_sb, y_m2, p, x, mod, g2.reshape(1, d), w_rw, w_sb, w_m2, w_o)


def _ffn_kernel(h_ref, x_ref, mod_ref, wu_ref, cw_ref, cb_ref, wd_ref, fg_ref, o_ref,
                ubuf, carry, *, final):
    tm = h_ref.shape[1]
    fc = FFN_COL_CHUNK

    @pl.when(pl.program_id(1) == 0)
    def _():
        carry[...] = jnp.zeros_like(carry)

    h = h_ref[0]
    acc = jnp.zeros((tm, D_MODEL), F32)
    for ci in range(D_FF // fc):
        halves = []
        for half in range(2):
            cols = slice(half * D_FF + ci * fc, half * D_FF + (ci + 1) * fc)
            u = _dot(h, wu_ref[:, cols])
            ubuf[0:SUBLANES, :] = carry[:, cols]
            ubuf[SUBLANES:SUBLANES + tm, :] = u
            carry[:, cols] = u[tm - SUBLANES:tm, :]
            conv = cb_ref[:, cols] + cw_ref[FFN_CONV - 1:FFN_CONV, cols] * u
            for i in range(FFN_CONV - 1):
                sh = FFN_CONV - 1 - i
                conv = conv + cw_ref[i:i + 1, cols] * ubuf[SUBLANES - sh:SUBLANES - sh + tm, :]
            halves.append(conv)
        act = _silu(halves[0]) * halves[1]
        acc = acc + _dot(act.astype(BF16), wd_ref[ci * fc:(ci + 1) * fc, :])
    xn = x_ref[0] + mod_ref[0, 5:6, :] * acc
    if final:
        xn = xn * lax.rsqrt(jnp.mean(xn * xn, axis=-1, keepdims=True) + NORM_EPS) * fg_ref[...]
    o_ref[0] = xn


def _ffn(h2, x, mod, w_up, conv_w, conv_b, w_down, final_g, tm, final):
    bsz, seq, d = x.shape
    const = lambda b, i: (0, 0)
    tok = pl.BlockSpec((1, tm, d), lambda b, i: (b, i, 0))
    resident = lambda shape: pl.BlockSpec(shape, const, pipeline_mode=pl.Buffered(1))
    return pl.pallas_call(
        functools.partial(_ffn_kernel, final=final),
        grid=(bsz, seq // tm),
        in_specs=[
            tok, tok,
            pl.BlockSpec((1, N_MOD, d), lambda b, i: (b, 0, 0)),
            resident((d, 2 * D_FF)),
            pl.BlockSpec((FFN_CONV, 2 * D_FF), const),
            pl.BlockSpec((1, 2 * D_FF), const),
            resident((D_FF, d)),
            pl.BlockSpec((1, d), const),
        ],
        out_specs=tok,
        out_shape=jax.ShapeDtypeStruct((bsz, seq, d), F32),
        scratch_shapes=[
            pltpu.VMEM((SUBLANES + tm, FFN_COL_CHUNK), F32),
            pltpu.VMEM((SUBLANES, 2 * D_FF), F32),
        ],
        compiler_params=pltpu.CompilerParams(
            dimension_semantics=("parallel", "arbitrary"), vmem_limit_bytes=VMEM_LIMIT),
        name="convffn",
    )(h2, x, mod, w_up, conv_w, conv_b.reshape(1, -1), w_down, final_g.reshape(1, d))


def _in_weight(w_in):
    rw, sb = RW_IN, 3 * SB_WIDTH
    z0 = rw + sb
    x0 = z0 + M2_WIDTH
    t0 = x0 + M2_CONV_DIM
    g0 = t0 + M2_HEADS
    d = w_in.shape[0]
    parts = [w_in[:, x0:t0], w_in[:, rw:z0], w_in[:, g0:g0 + 3 * D_MODEL], w_in[:, z0:x0],
             w_in[:, 0:rw], w_in[:, t0:g0],
             jnp.zeros((d, P_WIDTH - OFF_DT - M2_HEADS), w_in.dtype)]
    return jnp.concatenate(parts, axis=1).astype(BF16)


def _rwkv_lora_weight(w2, a2, g2):
    wl = jnp.zeros((RW_LORA, 3 * RW_WIDTH), F32)
    wl = wl.at[0:64, 0:RW_WIDTH].set(w2)
    wl = wl.at[64:128, RW_WIDTH:2 * RW_WIDTH].set(a2)
    wl = wl.at[128:256, 2 * RW_WIDTH:3 * RW_WIDTH].set(g2)
    return wl.astype(BF16)


def _rows(vectors, width, rows=SUBLANES):
    out = jnp.zeros((rows, width), F32)
    for i, vec in enumerate(vectors):
        out = out.at[i, 0:vec.shape[0]].set(vec.astype(F32))
    return out


def kernel(x, c, ada_w, ada_b, norm1_g, norm2_g, w_in, rw_mu, rw_w0, rw_w2, rw_a0, rw_a2,
           rw_g2, rw_k_k, rw_k_a, rw_r_k, rw_ln_g, rw_ln_b, rw_wo, sb_wo, m2_conv_w,
           m2_conv_b, m2_dt_bias, m2_a_log, m2_d, m2_norm_g, m2_wo, w_out, ffn_w_up,
           ffn_conv_w, ffn_conv_b, ffn_w_down, final_norm_g):
    bsz, seq, d = x.shape
    depth = ada_w.shape[0]
    assert d == D_MODEL and seq % SB_BLOCK == 0
    tm_in = min(1024, seq)
    tm_merge = min(512, seq)
    tm_ffn = min(512, seq)

    head = jnp.arange(RW_UNIT_LANES) // HEAD_DIM
    same_head = (head[:, None] == head[None, :]).astype(BF16)
    lane_head = jnp.arange(M2_WIDTH) // HEAD_DIM
    expand = (jnp.arange(LANES)[:, None] == lane_head[None, :]).astype(BF16)

    mod_all = _modulation(c, ada_w, ada_b)
    for l in range(depth):
        mod = mod_all[l].reshape(bsz, N_MOD, d)
        p = _inproj(x, mod, norm1_g[l], _in_weight(w_in[l]), tm_in)
        rw_vecs = _rows([rw_w0[l], rw_a0[l], rw_k_k[l], rw_k_a[l], rw_r_k[l].reshape(-1),
                         rw_ln_g[l], rw_ln_b[l]], RW_WIDTH)
        y_rw = _rwkv(p, rw_mu[l].reshape(1, RW_IN),
                     _rwkv_lora_weight(rw_w2[l], rw_a2[l], rw_g2[l]), rw_vecs, same_head)
        y_sb = _sb_attention(p)
        head_vecs = _rows([m2_dt_bias[l], m2_a_log[l]], LANES)
        dn_vecs = _rows([jnp.repeat(m2_d[l], HEAD_DIM), m2_norm_g[l]], M2_WIDTH)
        y_m2 = _mamba(p, m2_conv_w[l], m2_conv_b[l].reshape(1, -1), head_vecs, dn_vecs, expand)
        x, h2 = _merge(y_rw, y_sb, y_m2, p, x, mod, norm2_g[l], rw_wo[l].astype(BF16),
                       sb_wo[l].astype(BF16), m2_wo[l].astype(BF16), w_out[l].astype(BF16),
                       tm_merge)
        x = _ffn(h2, x, mod, ffn_w_up[l].astype(BF16), ffn_conv_w[l], ffn_conv_b[l],
                 ffn_w_down[l].astype(BF16), final_norm_g, tm_ffn, final=(l == depth - 1))
    return x
```

```python
import functools

import jax
import jax.numpy as jnp
from jax import lax
from jax.experimental import pallas as pl
from jax.experimental.pallas import tpu as pltpu

F32 = jnp.float32
BF16 = jnp.bfloat16
ACT_DTYPE = BF16

D_MODEL = 1024
NORM_EPS = 1e-6
N_MOD = 6

RW_HEADS = 8
HEAD_DIM = 64
RW_WIDTH = RW_HEADS * HEAD_DIM
RW_LORA = 256
RW_IN = 3 * RW_WIDTH + RW_LORA
RW_GN_EPS = 64e-5
RW_CHUNK = 64
RW_STEP_CHUNKS = 4
RW_UNIT_LANES = 256

SB_WIDTH = 512
SB_BLOCK = 256
SB_LANES = 256
SB_MASKED = -1e30

M2_HEADS = 16
M2_WIDTH = M2_HEADS * HEAD_DIM
M2_STATE = 128
M2_GROUPS = 2
M2_GROUP_WIDTH = M2_WIDTH // M2_GROUPS
M2_CONV = 4
M2_CHUNK = 128
M2_CONV_HALO = 16
M2_CONV_DIM = M2_WIDTH + 2 * M2_GROUPS * M2_STATE

D_FF = 2816
FFN_CONV = 3
FFN_COL_CHUNK = 1408

LANES = 128
SUBLANES = 8

OFF_XBC = 0
OFF_SB = OFF_XBC + M2_CONV_DIM
OFF_GATE = OFF_SB + 3 * SB_WIDTH
OFF_Z = OFF_GATE + 3 * D_MODEL
OFF_RW = OFF_Z + M2_WIDTH
OFF_DT = OFF_RW + RW_IN
P_WIDTH = 9216
IN_COL_TILE = 1024

VMEM_LIMIT = 56 * 1024 * 1024


def _dot(a, b):
    return jnp.dot(a, b, preferred_element_type=F32)


def _dot_nt(a, b):
    return lax.dot_general(a, b, (((1,), (1,)), ((), ())), preferred_element_type=F32)


def _dot_tn(a, b):
    return lax.dot_general(a, b, (((0,), (0,)), ((), ())), preferred_element_type=F32)


def _split(x):
    hi = x.astype(BF16)
    lo = (x - hi.astype(F32)).astype(BF16)
    return hi, lo


def _dot_exact_rhs(x, m):
    hi, lo = _split(x)
    return _dot(hi, m) + _dot(lo, m)


def _dot_exact_lhs(m, x):
    hi, lo = _split(x)
    return _dot(m, hi) + _dot(m, lo)


def _sigmoid(x):
    return 1.0 / (1.0 + jnp.exp(-x))


def _silu(x):
    return x * _sigmoid(x)


def _softplus(x):
    return jnp.maximum(x, 0.0) + jnp.log(1.0 + jnp.exp(-jnp.abs(x)))


def _mod_kernel(c_ref, w_ref, b_ref, o_ref):
    c_act = _silu(c_ref[...]).astype(BF16)
    o_ref[0] = _dot(c_act, w_ref[0].astype(BF16)) + b_ref[0]


def _modulation(c, ada_w, ada_b):
    depth, d, n = ada_w.shape
    bsz = c.shape[0]
    tile = 1024
    return pl.pallas_call(
        _mod_kernel,
        grid=(depth, n // tile),
        in_specs=[
            pl.BlockSpec((bsz, d), lambda l, j: (0, 0)),
            pl.BlockSpec((1, d, tile), lambda l, j: (l, 0, j)),
            pl.BlockSpec((1, 1, tile), lambda l, j: (l, 0, j)),
        ],
        out_specs=pl.BlockSpec((1, bsz, tile), lambda l, j: (l, 0, j)),
        out_shape=jax.ShapeDtypeStruct((depth, bsz, n), F32),
        compiler_params=pltpu.CompilerParams(
            dimension_semantics=("parallel", "parallel"), vmem_limit_bytes=VMEM_LIMIT),
        name="ada_mod",
    )(c, ada_w, ada_b.reshape(depth, 1, n))


def _rms_mod(x, g, scale, shift):
    y = x * lax.rsqrt(jnp.mean(x * x, axis=-1, keepdims=True) + NORM_EPS) * g
    return y * (1.0 + scale) + shift


def _inproj_kernel(x_ref, mod_ref, g_ref, w_ref, o_ref, h_scr):
    @pl.when(pl.program_id(2) == 0)
    def _():
        h = _rms_mod(x_ref[0], g_ref[...], mod_ref[0, 1:2, :], mod_ref[0, 0:1, :])
        h_scr[...] = h.astype(BF16)

    o_ref[0] = _dot(h_scr[...], w_ref[...]).astype(o_ref.dtype)


def _inproj(x, mod, g, wcat, layer, tm):
    bsz, seq, d = x.shape
    return pl.pallas_call(
        _inproj_kernel,
        grid=(bsz, seq // tm, P_WIDTH // IN_COL_TILE),
        in_specs=[
            pl.BlockSpec((1, tm, d), lambda b, i, j: (b, i, 0)),
            pl.BlockSpec((1, N_MOD, d), lambda b, i, j: (b, 0, 0)),
            pl.BlockSpec((1, d), lambda b, i, j: (0, 0)),
            pl.BlockSpec((None, d, IN_COL_TILE), lambda b, i, j: (layer, 0, j)),
        ],
        out_specs=pl.BlockSpec((1, tm, IN_COL_TILE), lambda b, i, j: (b, i, j)),
        out_shape=jax.ShapeDtypeStruct((bsz, seq, P_WIDTH), ACT_DTYPE),
        scratch_shapes=[pltpu.VMEM((tm, d), BF16)],
        compiler_params=pltpu.CompilerParams(
            dimension_semantics=("parallel", "parallel", "arbitrary"),
            vmem_limit_bytes=VMEM_LIMIT),
        name="inproj",
    )(x, mod, g.reshape(1, d), wcat)


def _pair_stack(x, m0):
    return jnp.concatenate([jnp.where(m0, x, 0.0), jnp.where(m0, 0.0, x)], axis=0)


def _side_by_side(x, left):
    return jnp.concatenate([jnp.where(left, x, 0), jnp.where(left, 0, x)], axis=0)


def _unit_lower_inverse(mats, block, left):
    n = mats[0].shape[0]
    row = lax.broadcasted_iota(jnp.int32, (n, 2 * n), 0)
    col = lax.broadcasted_iota(jnp.int32, (n, 2 * n), 1) & (n - 1)

    def quadrant(l):
        same = (row >> l) == (col >> l)
        low = ((row >> (l - 1)) & 1) == 1
        first = ((col >> (l - 1)) & 1) == 0
        return same & low & first

    eye = jnp.where(row == col, 1.0, 0.0)
    first_level = quadrant(1)
    inv = [eye + jnp.where(first_level, a, 0.0) for a in mats]
    level = 2
    while (1 << level) <= block:
        quad = quadrant(level)
        inv_b = [x.astype(BF16) for x in inv]
        t = [_dot(jnp.where(quad, a, 0.0).astype(BF16), _side_by_side(xb, left))
             for a, xb in zip(mats, inv_b)]
        inv = [x + _dot(xb, _side_by_side(tt.astype(BF16), left))
               for x, xb, tt in zip(inv, inv_b, t)]
        level += 1
    return inv


def _rwkv_kernel(p_ref, mu_ref, wl_ref, vec_ref, bd_ref, o_ref, pbuf, state):
    cn = RW_CHUNK
    tb = RW_STEP_CHUNKS * cn
    n_units = RW_WIDTH // RW_UNIT_LANES
    c = pl.program_id(1)

    @pl.when(c == 0)
    def _():
        pbuf[0:SUBLANES, :] = jnp.zeros((SUBLANES, RW_IN), F32)
        state[...] = jnp.zeros_like(state)

    p = p_ref[0].astype(F32)
    pbuf[SUBLANES:SUBLANES + tb, :] = p
    prev = pbuf[SUBLANES - 1:SUBLANES - 1 + tb, :]
    pbuf[0:SUBLANES, :] = p[tb - SUBLANES:tb, :]
    pm = p + (prev - p) * mu_ref[...]

    w0 = vec_ref[0:1, :]
    a0 = vec_ref[1:2, :]
    k_k = vec_ref[2:3, :]
    k_a = vec_ref[3:4, :]
    r_k = vec_ref[4:5, :]
    ln_g = vec_ref[5:6, :]
    ln_b = vec_ref[6:7, :]
    bd = bd_ref[...]

    def head_sum(x):
        wl = RW_UNIT_LANES
        return jnp.concatenate([_dot(x[:, w * wl:(w + 1) * wl].astype(BF16), bd)
                                for w in range(n_units)], axis=1)

    r = pm[:, 0:RW_WIDTH]
    k = pm[:, RW_WIDTH:2 * RW_WIDTH]
    v = pm[:, 2 * RW_WIDTH:3 * RW_WIDTH]
    lo = pm[:, 3 * RW_WIDTH:RW_IN]
    lane = lax.broadcasted_iota(jnp.int32, lo.shape, 1)
    act = jnp.where(lane < 64, jnp.tanh(lo), jnp.where(lane < 128, lo, _sigmoid(lo)))
    proj = _dot(act.astype(BF16), wl_ref[...])

    log_w = -_softplus(-(w0 + proj[:, 0:RW_WIDTH])) - 0.5
    ld = -jnp.exp(log_w)
    a = _sigmoid(a0 + proj[:, RW_WIDTH:2 * RW_WIDTH])
    g = proj[:, 2 * RW_WIDTH:3 * RW_WIDTH]

    kk = k * k_k
    kk = kk * lax.rsqrt(jnp.maximum(head_sum(kk * kk), 1e-24))
    k = k * (1.0 + (a - 1.0) * k_a)

    trow = lax.broadcasted_iota(jnp.int32, (tb, tb), 0)
    tcol = lax.broadcasted_iota(jnp.int32, (tb, tb), 1)
    shift = cn.bit_length() - 1
    same_chunk = (trow >> shift) == (tcol >> shift)
    ltri = jnp.where((trow >= tcol) & same_chunk, 1.0, 0.0).astype(BF16)
    cum = _dot_exact_lhs(ltri, ld)
    g_incl = jnp.exp(cum)
    g_inv = jnp.exp(-cum)
    al_bar = -kk * jnp.exp(cum - ld)
    r_bar = r * g_incl
    be_t = kk * a * g_inv
    k_t = k * g_inv

    wl = RW_UNIT_LANES
    n2 = 2 * cn
    lane_w = lax.broadcasted_iota(jnp.int32, (cn, wl), 1)
    first_head = (lane_w & HEAD_DIM) == 0
    left = lax.broadcasted_iota(jnp.int32, (n2, wl), 1) < LANES
    srow = lax.broadcasted_iota(jnp.int32, (n2, n2), 0)
    scol = lax.broadcasted_iota(jnp.int32, (n2, n2), 1)
    strict = srow > scol
    incl = srow >= scol
    krow = lax.broadcasted_iota(jnp.int32, (wl, wl), 0)
    kcol = lax.broadcasted_iota(jnp.int32, (wl, wl), 1)
    same_pair = (krow < LANES) == (kcol < LANES)

    units = [(ci, w) for ci in range(RW_STEP_CHUNKS) for w in range(n_units)]

    def stacked(x, ci, w):
        return _pair_stack(x[ci * cn:(ci + 1) * cn, w * wl:(w + 1) * wl], first_head)

    al_b = [stacked(al_bar, ci, w).astype(BF16) for ci, w in units]
    rb = [stacked(r_bar, ci, w) for ci, w in units]
    vs_b = [stacked(v, ci, w).astype(BF16) for ci, w in units]
    bt = [stacked(be_t, ci, w) for ci, w in units]
    kt = [stacked(k_t, ci, w) for ci, w in units]
    g_end = [jnp.exp(cum[(ci + 1) * cn - 1:(ci + 1) * cn, w * wl:(w + 1) * wl])
             for ci, w in units]

    lhs = [jnp.concatenate([x, y.astype(BF16)], axis=0) for x, y in zip(al_b, rb)]
    rhs = [jnp.concatenate([x, y], axis=0).astype(BF16) for x, y in zip(bt, kt)]
    gram = [(_dot_nt(x[:, 0:LANES], y[:, 0:LANES]), _dot_nt(x[:, LANES:wl], y[:, LANES:wl]))
            for x, y in zip(lhs, rhs)]

    def blocks(gm, rows, cols, mask):
        return jnp.concatenate([jnp.where(mask, m[rows, cols], 0.0) for m in gm], axis=1)

    top, bottom = slice(0, n2), slice(n2, 2 * n2)
    a_mat = [blocks(gm, top, top, strict) for gm in gram]
    b_mat = [blocks(gm, top, bottom, strict).astype(BF16) for gm in gram]
    arb = [blocks(gm, bottom, top, incl).astype(BF16) for gm in gram]
    ark = [blocks(gm, bottom, bottom, incl).astype(BF16) for gm in gram]
    inv_b = [x.astype(BF16) for x in _unit_lower_inverse(a_mat, cn, left)]
    vs_sbs = [_side_by_side(x, left) for x in vs_b]
    a_til = [_dot(nb, _side_by_side(x, left)).astype(BF16) for nb, x in zip(inv_b, al_b)]
    bv = [_dot(m, x) for m, x in zip(b_mat, vs_sbs)]
    u0 = [_dot(nb, _side_by_side(x.astype(BF16), left)).astype(BF16)
          for nb, x in zip(inv_b, bv)]
    r_til = [(x + _dot(m, _side_by_side(y, left))).astype(BF16)
             for x, m, y in zip(rb, arb, a_til)]
    y0 = [_dot(m1, _side_by_side(x, left)) + _dot(m2, y)
          for m1, x, m2, y in zip(arb, u0, ark, vs_sbs)]
    bt_end = [(x * ge).astype(BF16) for x, ge in zip(bt, g_end)]
    kt_end = [(x * ge).astype(BF16) for x, ge in zip(kt, g_end)]
    p_mat = [jnp.where(same_pair, _dot_tn(x, y), 0.0).astype(BF16) for x, y in zip(a_til, bt_end)]
    q_full = [_dot_tn(jnp.concatenate([x, y], axis=0), jnp.concatenate([z, t], axis=0))
              for x, y, z, t in zip(u0, vs_b, bt_end, kt_end)]
    q_mat = [jnp.concatenate([m[0:LANES, 0:LANES], m[LANES:wl, LANES:wl]], axis=1)
             for m in q_full]

    s = [state[w] for w in range(n_units)]
    y_rows = []
    for ci in range(RW_STEP_CHUNKS):
        ys = []
        for w in range(n_units):
            un = ci * n_units + w
            s_b = s[w].astype(BF16)
            y_ps = _dot_nt(r_til[un], _side_by_side(s_b, left)) + y0[un]
            ys.append(y_ps[0:cn] + y_ps[cn:n2])
            s[w] = s[w] * g_end[un] + _dot(s_b, p_mat[un]) + q_mat[un]
        y_rows.append(jnp.concatenate(ys, axis=1))
    for w in range(n_units):
        state[w] = s[w]
    y = jnp.concatenate(y_rows, axis=0)

    inv_n = 1.0 / HEAD_DIM
    mean = head_sum(y) * inv_n
    yc = y - mean
    var = head_sum(yc * yc) * inv_n
    y = yc * lax.rsqrt(var + RW_GN_EPS) * ln_g + ln_b
    y = y + head_sum(r * k * r_k) * v
    o_ref[0] = (y * g).astype(o_ref.dtype)


def _rwkv(p, mu, wl, vecs, bd):
    bsz, seq, _ = p.shape
    cn = RW_STEP_CHUNKS * RW_CHUNK
    const = lambda b, c: (0, 0)
    return pl.pallas_call(
        _rwkv_kernel,
        grid=(bsz, seq // cn),
        in_specs=[
            pl.BlockSpec((1, cn, RW_IN), lambda b, c: (b, c, OFF_RW // RW_IN)),
            pl.BlockSpec((1, RW_IN), const),
            pl.BlockSpec((RW_LORA, 3 * RW_WIDTH), const),
            pl.BlockSpec((SUBLANES, RW_WIDTH), const),
            pl.BlockSpec((RW_UNIT_LANES, RW_UNIT_LANES), const),
        ],
        out_specs=pl.BlockSpec((1, cn, RW_WIDTH), lambda b, c: (b, c, 0)),
        out_shape=jax.ShapeDtypeStruct((bsz, seq, RW_WIDTH), ACT_DTYPE),
        scratch_shapes=[
            pltpu.VMEM((SUBLANES + cn, RW_IN), F32),
            pltpu.VMEM((RW_WIDTH // RW_UNIT_LANES, LANES, RW_UNIT_LANES), F32),
        ],
        compiler_params=pltpu.CompilerParams(
            dimension_semantics=("parallel", "arbitrary"), vmem_limit_bytes=VMEM_LIMIT),
        name="rwkv7",
    )(p, mu, wl, vecs, bd)


def _sb_kernel(q_ref, k_ref, v_ref, o_ref, lb_scr, l1m_scr):
    tb = SB_BLOCK
    n_pairs = SB_LANES // LANES
    qi = pl.program_id(2)
    q = q_ref[0] * (HEAD_DIM ** -0.5)
    m0 = lax.broadcasted_iota(jnp.int32, (tb, LANES), 1) < HEAD_DIM
    q_stacks = [_pair_stack(q[:, pr * LANES:(pr + 1) * LANES], m0).astype(BF16)
                for pr in range(n_pairs)]
    row = lax.broadcasted_iota(jnp.int32, (2 * tb, tb), 0)
    col = lax.broadcasted_iota(jnp.int32, (2 * tb, tb), 1)
    causal = col < jnp.where(row >= tb, row - tb, row)
    krow = lax.broadcasted_iota(jnp.int32, (tb, tb), 0)
    kcol = lax.broadcasted_iota(jnp.int32, (tb, tb), 1)
    later_keys = jnp.where(krow > kcol, 1.0, 0.0).astype(BF16)

    def scores(j, slot, diag):
        start = pl.multiple_of(j * tb, tb)
        sums = []
        for pr in range(n_pairs):
            k_blk = k_ref[0, pl.ds(start, tb), pr * LANES:(pr + 1) * LANES].astype(BF16)
            z = _dot_nt(q_stacks[pr], k_blk)
            log_beta = jnp.minimum(z, 0.0) - jnp.log(1.0 + jnp.exp(-jnp.abs(z)))
            log_1m = log_beta - z
            if diag:
                log_1m = jnp.where(causal, log_1m, 0.0)
                log_beta = jnp.where(causal, log_beta, SB_MASKED)
            lb_scr[slot, pr] = log_beta
            l1m_scr[slot, pr] = log_1m.astype(BF16)
            sums.append(jnp.sum(log_1m, axis=-1, keepdims=True))
        return tuple(sums)

    def weights(j, slot, run, acc):
        start = pl.multiple_of(j * tb, tb)
        out = []
        for pr in range(n_pairs):
            v_blk = v_ref[0, pl.ds(start, tb), pr * LANES:(pr + 1) * LANES].astype(BF16)
            later = _dot(l1m_scr[slot, pr], later_keys) + run[pr]
            att = jnp.exp(lb_scr[slot, pr] + later)
            out.append(acc[pr] + _dot(att.astype(BF16), v_blk))
        return tuple(out)

    def step(i, carry):
        acc, run, sums = carry
        slot = i & 1
        new_sums = scores(qi - 1 - i, 1 - slot, False)
        acc = weights(qi - i, slot, run, acc)
        run = tuple(r + s for r, s in zip(run, sums))
        return acc, run, new_sums

    acc0 = tuple(jnp.zeros((2 * tb, LANES), F32) for _ in range(n_pairs))
    run0 = tuple(jnp.zeros((2 * tb, 1), F32) for _ in range(n_pairs))
    acc, run, _ = lax.fori_loop(0, qi, step, (acc0, run0, scores(qi, 0, True)))
    acc = weights(0, qi & 1, run, acc)
    o_ref[0] = jnp.concatenate(
        [jnp.where(m0, a[0:tb], a[tb:2 * tb]) for a in acc], axis=1).astype(o_ref.dtype)


def _sb_attention(p):
    bsz, seq, _ = p.shape
    tb = SB_BLOCK
    q0 = OFF_SB // SB_LANES
    k0 = (OFF_SB + SB_WIDTH) // SB_LANES
    v0 = (OFF_SB + 2 * SB_WIDTH) // SB_LANES
    return pl.pallas_call(
        _sb_kernel,
        grid=(bsz, SB_WIDTH // SB_LANES, seq // tb),
        in_specs=[
            pl.BlockSpec((1, tb, SB_LANES), lambda b, h, i: (b, i, q0 + h)),
            pl.BlockSpec((1, seq, SB_LANES), lambda b, h, i: (b, 0, k0 + h)),
            pl.BlockSpec((1, seq, SB_LANES), lambda b, h, i: (b, 0, v0 + h)),
        ],
        out_specs=pl.BlockSpec((1, tb, SB_LANES), lambda b, h, i: (b, i, h)),
        out_shape=jax.ShapeDtypeStruct((bsz, seq, SB_WIDTH), ACT_DTYPE),
        scratch_shapes=[
            pltpu.VMEM((2, SB_LANES // LANES, 2 * tb, tb), F32),
            pltpu.VMEM((2, SB_LANES // LANES, 2 * tb, tb), BF16),
        ],
        compiler_params=pltpu.CompilerParams(
            dimension_semantics=("parallel", "parallel", "arbitrary"),
            vmem_limit_bytes=VMEM_LIMIT),
        name="stickbreak",
    )(p, p, p)


def _mamba_kernel(xbc_ref, z_ref, dt_ref, cw_ref, cb_ref, hv_ref, dn_ref, ex_ref, o_ref,
                  xpad, state):
    ln = M2_CHUNK
    c = pl.program_id(1)

    halo = M2_CONV_HALO

    @pl.when(c == 0)
    def _():
        xpad[0:halo, :] = jnp.zeros((halo, M2_CONV_DIM), xpad.dtype)
        state[...] = jnp.zeros_like(state)

    x = xbc_ref[0]
    xpad[halo:halo + ln, :] = x
    xp = xpad[...]
    xpad[0:halo, :] = x[ln - halo:ln, :]
    trow = lax.broadcasted_iota(jnp.int32, (ln, halo + ln), 0)
    tcol = lax.broadcasted_iota(jnp.int32, (ln, halo + ln), 1)
    conv = cb_ref[...] + cw_ref[M2_CONV - 1:M2_CONV, :] * x.astype(F32)
    for i in range(M2_CONV - 1):
        sh = M2_CONV - 1 - i
        pick = jnp.where(tcol == trow + (halo - sh), 1.0, 0.0).astype(xp.dtype)
        conv = conv + cw_ref[i:i + 1, :] * _dot(pick, xp)
    xa = _silu(conv)
    xs = xa[:, 0:M2_WIDTH]

    dt = _softplus(dt_ref[0].astype(F32) + hv_ref[0:1, :])
    log_a = dt * (-jnp.exp(hv_ref[1:2, :]))
    row = lax.broadcasted_iota(jnp.int32, (ln, ln), 0)
    col = lax.broadcasted_iota(jnp.int32, (ln, ln), 1)
    lower = row >= col
    ltri = jnp.where(lower, 1.0, 0.0).astype(BF16)
    acum = _dot_exact_lhs(ltri, log_a)
    acum_t = acum.T
    ex = ex_ref[...]
    dt_x = _dot_exact_rhs(dt, ex)
    acum_x = _dot_exact_rhs(acum, ex)
    xdt = xs * dt_x

    m0 = lax.broadcasted_iota(jnp.int32, (ln, LANES), 1) < HEAD_DIM
    gw = M2_GROUP_WIDTH
    heads_per_group = M2_HEADS // M2_GROUPS
    y_groups = []
    for gi in range(M2_GROUPS):
        b_g = xa[:, M2_WIDTH + gi * M2_STATE:M2_WIDTH + (gi + 1) * M2_STATE].astype(BF16)
        c_off = M2_WIDTH + M2_GROUPS * M2_STATE
        c_g = xa[:, c_off + gi * M2_STATE:c_off + (gi + 1) * M2_STATE].astype(BF16)
        cb = _dot_nt(c_g, b_g)
        gs = slice(gi * gw, (gi + 1) * gw)
        ac_g = acum_x[:, gs]
        st = state[gi]
        y_off = _dot(c_g, st.astype(BF16)) * jnp.exp(ac_g)
        pairs = []
        for q in range(heads_per_group // 2):
            mats = []
            for hh in range(2):
                h = gi * heads_per_group + 2 * q + hh
                seg = acum[:, h:h + 1] - acum_t[h:h + 1, :]
                mats.append(cb * jnp.exp(jnp.where(lower, seg, -1e30)))
            xp = xdt[:, gi * gw + q * LANES:gi * gw + (q + 1) * LANES]
            pairs.append(_dot(jnp.concatenate(mats, axis=1).astype(BF16),
                              _pair_stack(xp, m0).astype(BF16)))
        y_diag = jnp.concatenate(pairs, axis=1)
        ac_end = ac_g[ln - 1:ln, :]
        to_end = jnp.exp(ac_end - ac_g)
        state[gi] = st * jnp.exp(ac_end) + _dot_tn(b_g, (xdt[:, gs] * to_end).astype(BF16))
        y = y_diag + y_off + dn_ref[0:1, gs] * xs[:, gs]
        y = y * _silu(z_ref[0, :, gs].astype(F32))
        y = y * lax.rsqrt(jnp.mean(y * y, axis=-1, keepdims=True) + NORM_EPS)
        y_groups.append(y * dn_ref[1:2, gs])
    o_ref[0] = jnp.concatenate(y_groups, axis=1).astype(o_ref.dtype)


def _mamba(p, conv_w, conv_b, head_vecs, dn_vecs, expand):
    bsz, seq, _ = p.shape
    ln = M2_CHUNK
    const = lambda b, c: (0, 0)
    return pl.pallas_call(
        _mamba_kernel,
        grid=(bsz, seq // ln),
        in_specs=[
            pl.BlockSpec((1, ln, M2_CONV_DIM), lambda b, c: (b, c, OFF_XBC // M2_CONV_DIM)),
            pl.BlockSpec((1, ln, M2_WIDTH), lambda b, c: (b, c, OFF_Z // M2_WIDTH)),
            pl.BlockSpec((1, ln, LANES), lambda b, c: (b, c, OFF_DT // LANES)),
            pl.BlockSpec((M2_CONV, M2_CONV_DIM), const),
            pl.BlockSpec((1, M2_CONV_DIM), const),
            pl.BlockSpec((SUBLANES, LANES), const),
            pl.BlockSpec((SUBLANES, M2_WIDTH), const),
            pl.BlockSpec((LANES, M2_WIDTH), const),
        ],
        out_specs=pl.BlockSpec((1, ln, M2_WIDTH), lambda b, c: (b, c, 0)),
        out_shape=jax.ShapeDtypeStruct((bsz, seq, M2_WIDTH), ACT_DTYPE),
        scratch_shapes=[
            pltpu.VMEM((M2_CONV_HALO + ln, M2_CONV_DIM), ACT_DTYPE),
            pltpu.VMEM((M2_GROUPS, M2_STATE, M2_GROUP_WIDTH), F32),
        ],
        compiler_params=pltpu.CompilerParams(
            dimension_semantics=("parallel", "arbitrary"), vmem_limit_bytes=VMEM_LIMIT),
        name="mamba2",
    )(p, p, p, conv_w, conv_b, head_vecs, dn_vecs, expand)


def _merge_kernel(yrw_ref, ysb_ref, ym2_ref, pg_ref, x_ref, mod_ref, g2_ref,
                  wrw_ref, wsb_ref, wm2_ref, wo_ref, xo_ref, h_ref):
    d = D_MODEL
    pg = pg_ref[0].astype(F32)
    merged = _sigmoid(pg[:, 0:d]) * _dot(yrw_ref[0].astype(BF16), wrw_ref[...])
    merged = merged + _sigmoid(pg[:, d:2 * d]) * _dot(ysb_ref[0].astype(BF16), wsb_ref[...])
    merged = merged + _sigmoid(pg[:, 2 * d:3 * d]) * _dot(ym2_ref[0].astype(BF16), wm2_ref[...])
    xn = x_ref[0] + mod_ref[0, 2:3, :] * _dot(merged.astype(BF16), wo_ref[...])
    xo_ref[0] = xn
    h_ref[0] = _rms_mod(xn, g2_ref[...], mod_ref[0, 4:5, :], mod_ref[0, 3:4, :]).astype(BF16)


def _merge(y_rw, y_sb, y_m2, p, x, mod, g2, w_rw, w_sb, w_m2, w_o, layer, tm):
    bsz, seq, d = x.shape
    const = lambda b, i: (0, 0)
    tok = lambda w: pl.BlockSpec((1, tm, w), lambda b, i: (b, i, 0))
    weight = lambda k: pl.BlockSpec((None, k, d), lambda b, i: (layer, 0, 0))
    return pl.pallas_call(
        _merge_kernel,
        grid=(bsz, seq // tm),
        in_specs=[
            tok(RW_WIDTH), tok(SB_WIDTH), tok(M2_WIDTH),
            pl.BlockSpec((1, tm, 3 * d), lambda b, i: (b, i, OFF_GATE // (3 * d))),
            tok(d),
            pl.BlockSpec((1, N_MOD, d), lambda b, i: (b, 0, 0)),
            pl.BlockSpec((1, d), const),
            weight(RW_WIDTH), weight(SB_WIDTH), weight(M2_WIDTH), weight(d),
        ],
        out_specs=[tok(d), tok(d)],
        out_shape=[jax.ShapeDtypeStruct((bsz, seq, d), F32),
                   jax.ShapeDtypeStruct((bsz, seq, d), BF16)],
        compiler_params=pltpu.CompilerParams(
            dimension_semantics=("parallel", "parallel"), vmem_limit_bytes=VMEM_LIMIT),
        name="merge",
    )(y_rw, y_sb, y_m2, p, x, mod, g2.reshape(1, d), w_rw, w_sb, w_m2, w_o)


def _ffn_kernel(h_ref, x_ref, mod_ref, wu_ref, cw_ref, cb_ref, wd_ref, fg_ref, o_ref,
                ubuf, carry, *, final):
    tm = h_ref.shape[1]
    fc = FFN_COL_CHUNK

    @pl.when(pl.program_id(1) == 0)
    def _():
        carry[...] = jnp.zeros_like(carry)

    h = h_ref[0]
    acc = jnp.zeros((tm, D_MODEL), F32)
    for ci in range(D_FF // fc):
        halves = []
        for half in range(2):
            cols = slice(half * D_FF + ci * fc, half * D_FF + (ci + 1) * fc)
            u = _dot(h, wu_ref[:, cols])
            ubuf[0:SUBLANES, :] = carry[:, cols]
            ubuf[SUBLANES:SUBLANES + tm, :] = u
            carry[:, cols] = u[tm - SUBLANES:tm, :]
            conv = cb_ref[:, cols] + cw_ref[FFN_CONV - 1:FFN_CONV, cols] * u
            for i in range(FFN_CONV - 1):
                sh = FFN_CONV - 1 - i
                conv = conv + cw_ref[i:i + 1, cols] * ubuf[SUBLANES - sh:SUBLANES - sh + tm, :]
            halves.append(conv)
        act = _silu(halves[0]) * halves[1]
        acc = acc + _dot(act.astype(BF16), wd_ref[ci * fc:(ci + 1) * fc, :])
    xn = x_ref[0] + mod_ref[0, 5:6, :] * acc
    if final:
        xn = xn * lax.rsqrt(jnp.mean(xn * xn, axis=-1, keepdims=True) + NORM_EPS) * fg_ref[...]
    o_ref[0] = xn


def _ffn(h2, x, mod, w_up, conv_w, conv_b, w_down, final_g, layer, tm, final):
    bsz, seq, d = x.shape
    const = lambda b, i: (0, 0)
    tok = pl.BlockSpec((1, tm, d), lambda b, i: (b, i, 0))
    resident = lambda shape: pl.BlockSpec((None,) + shape, lambda b, i: (layer, 0, 0),
                                          pipeline_mode=pl.Buffered(1))
    return pl.pallas_call(
        functools.partial(_ffn_kernel, final=final),
        grid=(bsz, seq // tm),
        in_specs=[
            tok, tok,
            pl.BlockSpec((1, N_MOD, d), lambda b, i: (b, 0, 0)),
            resident((d, 2 * D_FF)),
            pl.BlockSpec((FFN_CONV, 2 * D_FF), const),
            pl.BlockSpec((1, 2 * D_FF), const),
            resident((D_FF, d)),
            pl.BlockSpec((1, d), const),
        ],
        out_specs=tok,
        out_shape=jax.ShapeDtypeStruct((bsz, seq, d), F32),
        scratch_shapes=[
            pltpu.VMEM((SUBLANES + tm, FFN_COL_CHUNK), F32),
            pltpu.VMEM((SUBLANES, 2 * D_FF), F32),
        ],
        compiler_params=pltpu.CompilerParams(
            dimension_semantics=("parallel", "arbitrary"), vmem_limit_bytes=VMEM_LIMIT),
        name="convffn",
    )(h2, x, mod, w_up, conv_w, conv_b.reshape(1, -1), w_down, final_g.reshape(1, d))


def _in_weight(w_in):
    rw, sb = RW_IN, 3 * SB_WIDTH
    z0 = rw + sb
    x0 = z0 + M2_WIDTH
    t0 = x0 + M2_CONV_DIM
    g0 = t0 + M2_HEADS
    w = w_in.astype(BF16)
    pad = jnp.zeros(w.shape[:-1] + (P_WIDTH - OFF_DT - M2_HEADS,), BF16)
    parts = [w[..., x0:t0], w[..., rw:z0], w[..., g0:g0 + 3 * D_MODEL], w[..., z0:x0],
             w[..., 0:rw], w[..., t0:g0], pad]
    return jnp.concatenate(parts, axis=-1)


def _rwkv_lora_weight(w2, a2, g2):
    n = RW_WIDTH
    return jnp.concatenate([jnp.pad(w2, ((0, 0), (0, 2 * n))),
                            jnp.pad(a2, ((0, 0), (n, n))),
                            jnp.pad(g2, ((0, 0), (2 * n, 0)))], axis=0).astype(BF16)


def _rows(vectors, width, rows=SUBLANES):
    padded = [jnp.pad(vec.astype(F32), (0, width - vec.shape[0])) for vec in vectors]
    padded += [jnp.zeros((width,), F32)] * (rows - len(vectors))
    return jnp.stack(padded)


def kernel(x, c, ada_w, ada_b, norm1_g, norm2_g, w_in, rw_mu, rw_w0, rw_w2, rw_a0, rw_a2,
           rw_g2, rw_k_k, rw_k_a, rw_r_k, rw_ln_g, rw_ln_b, rw_wo, sb_wo, m2_conv_w,
           m2_conv_b, m2_dt_bias, m2_a_log, m2_d, m2_norm_g, m2_wo, w_out, ffn_w_up,
           ffn_conv_w, ffn_conv_b, ffn_w_down, final_norm_g):
    bsz, seq, d = x.shape
    depth = ada_w.shape[0]
    assert d == D_MODEL and seq % SB_BLOCK == 0
    tm_in = min(2048, seq)
    tm_merge = min(512, seq)
    tm_ffn = min(512, seq)

    head = jnp.arange(RW_UNIT_LANES) // HEAD_DIM
    same_head = (head[:, None] == head[None, :]).astype(BF16)
    lane_head = jnp.arange(M2_WIDTH) // HEAD_DIM
    expand = (jnp.arange(LANES)[:, None] == lane_head[None, :]).astype(BF16)

    wcat = _in_weight(w_in)
    w_rw, w_sb, w_m2, w_o, w_up, w_down = (
        w.astype(BF16) for w in (rw_wo, sb_wo, m2_wo, w_out, ffn_w_up, ffn_w_down))

    mod_all = _modulation(c, ada_w, ada_b)
    for l in range(depth):
        mod = mod_all[l].reshape(bsz, N_MOD, d)
        p = _inproj(x, mod, norm1_g[l], wcat, l, tm_in)
        rw_vecs = _rows([rw_w0[l], rw_a0[l], rw_k_k[l], rw_k_a[l], rw_r_k[l].reshape(-1),
                         rw_ln_g[l], rw_ln_b[l]], RW_WIDTH)
        y_rw = _rwkv(p, rw_mu[l].reshape(1, RW_IN),
                     _rwkv_lora_weight(rw_w2[l], rw_a2[l], rw_g2[l]), rw_vecs, same_head)
        y_sb = _sb_attention(p)
        head_vecs = _rows([m2_dt_bias[l], m2_a_log[l]], LANES)
        dn_vecs = _rows([jnp.repeat(m2_d[l], HEAD_DIM), m2_norm_g[l]], M2_WIDTH)
        y_m2 = _mamba(p, m2_conv_w[l], m2_conv_b[l].reshape(1, -1), head_vecs, dn_vecs, expand)
        x, h2 = _merge(y_rw, y_sb, y_m2, p, x, mod, norm2_g[l], w_rw, w_sb, w_m2, w_o, l,
                       tm_merge)
        x = _ffn(h2, x, mod, w_up, ffn_conv_w[l], ffn_conv_b[l], w_down, final_norm_g, l,
                 tm_ffn, final=(l == depth - 1))
    return x
```

```python
import functools

import jax
import jax.numpy as jnp
from jax import lax
from jax.experimental import pallas as pl
from jax.experimental.pallas import tpu as pltpu

F32 = jnp.float32
BF16 = jnp.bfloat16
ACT_DTYPE = BF16

D_MODEL = 1024
NORM_EPS = 1e-6
N_MOD = 6

RW_HEADS = 8
HEAD_DIM = 64
RW_WIDTH = RW_HEADS * HEAD_DIM
RW_LORA = 256
RW_IN = 3 * RW_WIDTH + RW_LORA
RW_GN_EPS = 64e-5
RW_CHUNK = 64
RW_STEP_CHUNKS = 8
RW_UNIT_LANES = 256

SB_WIDTH = 512
SB_BLOCK = 256
SB_LANES = 512
SB_MASKED = -1e30

M2_HEADS = 16
M2_WIDTH = M2_HEADS * HEAD_DIM
M2_STATE = 128
M2_GROUPS = 2
M2_GROUP_WIDTH = M2_WIDTH // M2_GROUPS
M2_CONV = 4
M2_CHUNK = 128
M2_CONV_HALO = 16
M2_CONV_DIM = M2_WIDTH + 2 * M2_GROUPS * M2_STATE

D_FF = 2816
FFN_CONV = 3
FFN_COL_CHUNKS = (1280, 1536)

LANES = 128
SUBLANES = 8

OFF_RW = 0
OFF_SB = OFF_RW + RW_IN
OFF_Z = OFF_SB + 3 * SB_WIDTH
OFF_XBC = OFF_Z + M2_WIDTH
OFF_DT = OFF_XBC + M2_CONV_DIM
N_MAIN = OFF_DT + M2_HEADS
IN_COL_TILE = 1024
IN_MAIN_TILES = 6
IN_GATE_TILES = 3 * D_MODEL // IN_COL_TILE

VMEM_LIMIT = 56 * 1024 * 1024


def _dot(a, b):
    return jnp.dot(a, b, preferred_element_type=F32)


def _dot_nt(a, b):
    return lax.dot_general(a, b, (((1,), (1,)), ((), ())), preferred_element_type=F32)


def _dot_tn(a, b):
    return lax.dot_general(a, b, (((0,), (0,)), ((), ())), preferred_element_type=F32)


def _split(x):
    hi = x.astype(BF16)
    lo = (x - hi.astype(F32)).astype(BF16)
    return hi, lo


def _dot_exact_rhs(x, m):
    hi, lo = _split(x)
    return _dot(hi, m) + _dot(lo, m)


def _dot_exact_lhs(m, x):
    hi, lo = _split(x)
    return _dot(m, hi) + _dot(m, lo)


def _sigmoid(x):
    return 1.0 / (1.0 + jnp.exp(-x))


def _silu(x):
    return x * _sigmoid(x)


def _softplus(x):
    return jnp.maximum(x, 0.0) + jnp.log(1.0 + jnp.exp(-jnp.abs(x)))


def _mod_kernel(c_ref, w_ref, b_ref, o_ref):
    c_act = _silu(c_ref[...]).astype(BF16)
    o_ref[0] = _dot(c_act, w_ref[0].astype(BF16)) + b_ref[0]


def _modulation(c, ada_w, ada_b):
    depth, d, n = ada_w.shape
    bsz = c.shape[0]
    tile = 1024
    return pl.pallas_call(
        _mod_kernel,
        grid=(depth, n // tile),
        in_specs=[
            pl.BlockSpec((bsz, d), lambda l, j: (0, 0)),
            pl.BlockSpec((1, d, tile), lambda l, j: (l, 0, j)),
            pl.BlockSpec((1, 1, tile), lambda l, j: (l, 0, j)),
        ],
        out_specs=pl.BlockSpec((1, bsz, tile), lambda l, j: (l, 0, j)),
        out_shape=jax.ShapeDtypeStruct((depth, bsz, n), F32),
        compiler_params=pltpu.CompilerParams(
            dimension_semantics=("parallel", "parallel"), vmem_limit_bytes=VMEM_LIMIT),
        name="ada_mod",
    )(c, ada_w, ada_b.reshape(depth, 1, n))


def _rms_mod(x, g, scale, shift):
    y = x * lax.rsqrt(jnp.mean(x * x, axis=-1, keepdims=True) + NORM_EPS) * g
    return y * (1.0 + scale) + shift


def _inproj_kernel(x_ref, mod_ref, g_ref, wm_ref, wg_ref, om_ref, og_ref, h_scr):
    j = pl.program_id(2)

    @pl.when(j == 0)
    def _():
        h = _rms_mod(x_ref[0], g_ref[...], mod_ref[0, 1:2, :], mod_ref[0, 0:1, :])
        h_scr[...] = h.astype(BF16)

    @pl.when(j < IN_MAIN_TILES)
    def _():
        om_ref[0] = _dot(h_scr[...], wm_ref[...]).astype(om_ref.dtype)

    @pl.when(j >= IN_MAIN_TILES)
    def _():
        og_ref[0] = _dot(h_scr[...], wg_ref[...]).astype(og_ref.dtype)


def _inproj(x, mod, g, w_main, w_gate, layer, tm):
    bsz, seq, d = x.shape
    main_tile = lambda j: jnp.minimum(j, IN_MAIN_TILES - 1)
    gate_tile = lambda j: jnp.maximum(j - IN_MAIN_TILES, 0)
    return pl.pallas_call(
        _inproj_kernel,
        grid=(bsz, seq // tm, IN_MAIN_TILES + IN_GATE_TILES),
        in_specs=[
            pl.BlockSpec((1, tm, d), lambda b, i, j: (b, i, 0)),
            pl.BlockSpec((1, N_MOD, d), lambda b, i, j: (b, 0, 0)),
            pl.BlockSpec((1, d), lambda b, i, j: (0, 0)),
            pl.BlockSpec((None, d, IN_COL_TILE), lambda b, i, j: (layer, 0, main_tile(j))),
            pl.BlockSpec((None, d, IN_COL_TILE), lambda b, i, j: (layer, 0, gate_tile(j))),
        ],
        out_specs=[
            pl.BlockSpec((1, tm, IN_COL_TILE), lambda b, i, j: (b, i, main_tile(j))),
            pl.BlockSpec((1, tm, IN_COL_TILE), lambda b, i, j: (b, i, gate_tile(j))),
        ],
        out_shape=[
            jax.ShapeDtypeStruct((bsz, seq, IN_MAIN_TILES * IN_COL_TILE), ACT_DTYPE),
            jax.ShapeDtypeStruct((bsz, seq, IN_GATE_TILES * IN_COL_TILE), ACT_DTYPE),
        ],
        scratch_shapes=[pltpu.VMEM((tm, d), BF16)],
        compiler_params=pltpu.CompilerParams(
            dimension_semantics=("parallel", "parallel", "arbitrary"),
            vmem_limit_bytes=VMEM_LIMIT),
        name="inproj",
    )(x, mod, g.reshape(1, d), w_main, w_gate)


def _pair_stack(x, m0):
    return jnp.concatenate([jnp.where(m0, x, 0.0), jnp.where(m0, 0.0, x)], axis=0)


def _side_by_side(x, left):
    return jnp.concatenate([jnp.where(left, x, 0), jnp.where(left, 0, x)], axis=0)


def _unit_lower_inverse(mats, block, left):
    n = mats[0].shape[0]
    row = lax.broadcasted_iota(jnp.int32, (n, 2 * n), 0)
    col = lax.broadcasted_iota(jnp.int32, (n, 2 * n), 1) & (n - 1)

    def quadrant(l):
        same = (row >> l) == (col >> l)
        low = ((row >> (l - 1)) & 1) == 1
        first = ((col >> (l - 1)) & 1) == 0
        return same & low & first

    eye = jnp.where(row == col, 1.0, 0.0)
    first_level = quadrant(1)
    inv = [eye + jnp.where(first_level, a, 0.0) for a in mats]
    level = 2
    while (1 << level) <= block:
        quad = quadrant(level)
        inv_b = [x.astype(BF16) for x in inv]
        t = [_dot(jnp.where(quad, a, 0.0).astype(BF16), _side_by_side(xb, left))
             for a, xb in zip(mats, inv_b)]
        inv = [x + _dot(xb, _side_by_side(tt.astype(BF16), left))
               for x, xb, tt in zip(inv, inv_b, t)]
        level += 1
    return inv


def _rwkv_kernel(p_ref, mu_ref, wl_ref, vec_ref, bd_ref, o_ref, pbuf, state):
    cn = RW_CHUNK
    tb = RW_STEP_CHUNKS * cn
    n_units = RW_WIDTH // RW_UNIT_LANES
    c = pl.program_id(1)

    @pl.when(c == 0)
    def _():
        pbuf[0:SUBLANES, :] = jnp.zeros((SUBLANES, RW_IN), F32)
        state[...] = jnp.zeros_like(state)

    p = p_ref[0].astype(F32)
    pbuf[SUBLANES:SUBLANES + tb, :] = p
    prev = pbuf[SUBLANES - 1:SUBLANES - 1 + tb, :]
    pbuf[0:SUBLANES, :] = p[tb - SUBLANES:tb, :]
    pm = p + (prev - p) * mu_ref[...]

    w0 = vec_ref[0:1, :]
    a0 = vec_ref[1:2, :]
    k_k = vec_ref[2:3, :]
    k_a = vec_ref[3:4, :]
    r_k = vec_ref[4:5, :]
    ln_g = vec_ref[5:6, :]
    ln_b = vec_ref[6:7, :]
    bd = bd_ref[...]

    def head_sum(x):
        wl = RW_UNIT_LANES
        return jnp.concatenate([_dot(x[:, w * wl:(w + 1) * wl].astype(BF16), bd)
                                for w in range(n_units)], axis=1)

    r = pm[:, 0:RW_WIDTH]
    k = pm[:, RW_WIDTH:2 * RW_WIDTH]
    v = pm[:, 2 * RW_WIDTH:3 * RW_WIDTH]
    lo = pm[:, 3 * RW_WIDTH:RW_IN]
    lane = lax.broadcasted_iota(jnp.int32, lo.shape, 1)
    act = jnp.where(lane < 64, jnp.tanh(lo), jnp.where(lane < 128, lo, _sigmoid(lo)))
    proj = _dot(act.astype(BF16), wl_ref[...])

    log_w = -_softplus(-(w0 + proj[:, 0:RW_WIDTH])) - 0.5
    ld = -jnp.exp(log_w)
    a = _sigmoid(a0 + proj[:, RW_WIDTH:2 * RW_WIDTH])
    g = proj[:, 2 * RW_WIDTH:3 * RW_WIDTH]

    kk = k * k_k
    kk = kk * lax.rsqrt(jnp.maximum(head_sum(kk * kk), 1e-24))
    k = k * (1.0 + (a - 1.0) * k_a)

    trow = lax.broadcasted_iota(jnp.int32, (tb, tb), 0)
    tcol = lax.broadcasted_iota(jnp.int32, (tb, tb), 1)
    shift = cn.bit_length() - 1
    same_chunk = (trow >> shift) == (tcol >> shift)
    ltri = jnp.where((trow >= tcol) & same_chunk, 1.0, 0.0).astype(BF16)
    cum = _dot_exact_lhs(ltri, ld)
    g_incl = jnp.exp(cum)
    g_inv = jnp.exp(-cum)
    al_bar = -kk * jnp.exp(cum - ld)
    r_bar = r * g_incl
    be_t = kk * a * g_inv
    k_t = k * g_inv

    wl = RW_UNIT_LANES
    n2 = 2 * cn
    lane_w = lax.broadcasted_iota(jnp.int32, (cn, wl), 1)
    first_head = (lane_w & HEAD_DIM) == 0
    left = lax.broadcasted_iota(jnp.int32, (n2, wl), 1) < LANES
    srow = lax.broadcasted_iota(jnp.int32, (n2, n2), 0)
    scol = lax.broadcasted_iota(jnp.int32, (n2, n2), 1)
    strict = srow > scol
    incl = srow >= scol
    krow = lax.broadcasted_iota(jnp.int32, (wl, wl), 0)
    kcol = lax.broadcasted_iota(jnp.int32, (wl, wl), 1)
    same_pair = (krow < LANES) == (kcol < LANES)

    units = [(ci, w) for ci in range(RW_STEP_CHUNKS) for w in range(n_units)]

    def stacked(x, ci, w):
        return _pair_stack(x[ci * cn:(ci + 1) * cn, w * wl:(w + 1) * wl], first_head)

    al_b = [stacked(al_bar, ci, w).astype(BF16) for ci, w in units]
    rb = [stacked(r_bar, ci, w) for ci, w in units]
    vs_b = [stacked(v, ci, w).astype(BF16) for ci, w in units]
    bt = [stacked(be_t, ci, w) for ci, w in units]
    kt = [stacked(k_t, ci, w) for ci, w in units]
    g_end = [jnp.exp(cum[(ci + 1) * cn - 1:(ci + 1) * cn, w * wl:(w + 1) * wl])
             for ci, w in units]

    lhs = [jnp.concatenate([x, y.astype(BF16)], axis=0) for x, y in zip(al_b, rb)]
    rhs = [jnp.concatenate([x, y], axis=0).astype(BF16) for x, y in zip(bt, kt)]
    gram = [(_dot_nt(x[:, 0:LANES], y[:, 0:LANES]), _dot_nt(x[:, LANES:wl], y[:, LANES:wl]))
            for x, y in zip(lhs, rhs)]

    def blocks(gm, rows, cols, mask):
        return jnp.concatenate([jnp.where(mask, m[rows, cols], 0.0) for m in gm], axis=1)

    top, bottom = slice(0, n2), slice(n2, 2 * n2)
    a_mat = [blocks(gm, top, top, strict) for gm in gram]
    b_mat = [blocks(gm, top, bottom, strict).astype(BF16) for gm in gram]
    arb = [blocks(gm, bottom, top, incl).astype(BF16) for gm in gram]
    ark = [blocks(gm, bottom, bottom, incl).astype(BF16) for gm in gram]
    inv_b = [x.astype(BF16) for x in _unit_lower_inverse(a_mat, cn, left)]
    vs_sbs = [_side_by_side(x, left) for x in vs_b]
    a_til = [_dot(nb, _side_by_side(x, left)).astype(BF16) for nb, x in zip(inv_b, al_b)]
    bv = [_dot(m, x) for m, x in zip(b_mat, vs_sbs)]
    u0 = [_dot(nb, _side_by_side(x.astype(BF16), left)).astype(BF16)
          for nb, x in zip(inv_b, bv)]
    r_til = [(x + _dot(m, _side_by_side(y, left))).astype(BF16)
             for x, m, y in zip(rb, arb, a_til)]
    y0 = [_dot(m1, _side_by_side(x, left)) + _dot(m2, y)
          for m1, x, m2, y in zip(arb, u0, ark, vs_sbs)]
    bt_end = [(x * ge).astype(BF16) for x, ge in zip(bt, g_end)]
    kt_end = [(x * ge).astype(BF16) for x, ge in zip(kt, g_end)]
    p_mat = [jnp.where(same_pair, _dot_tn(x, y), 0.0).astype(BF16) for x, y in zip(a_til, bt_end)]
    q_full = [_dot_tn(jnp.concatenate([x, y], axis=0), jnp.concatenate([z, t], axis=0))
              for x, y, z, t in zip(u0, vs_b, bt_end, kt_end)]
    q_mat = [jnp.concatenate([m[0:LANES, 0:LANES], m[LANES:wl, LANES:wl]], axis=1)
             for m in q_full]

    s = [state[w] for w in range(n_units)]
    y_rows = []
    for ci in range(RW_STEP_CHUNKS):
        ys = []
        for w in range(n_units):
            un = ci * n_units + w
            s_b = s[w].astype(BF16)
            y_ps = _dot_nt(r_til[un], _side_by_side(s_b, left)) + y0[un]
            ys.append(y_ps[0:cn] + y_ps[cn:n2])
            s[w] = s[w] * g_end[un] + _dot(s_b, p_mat[un]) + q_mat[un]
        y_rows.append(jnp.concatenate(ys, axis=1))
    for w in range(n_units):
        state[w] = s[w]
    y = jnp.concatenate(y_rows, axis=0)

    inv_n = 1.0 / HEAD_DIM
    mean = head_sum(y) * inv_n
    yc = y - mean
    var = head_sum(yc * yc) * inv_n
    y = yc * lax.rsqrt(var + RW_GN_EPS) * ln_g + ln_b
    y = y + head_sum(r * k * r_k) * v
    o_ref[0] = (y * g).astype(o_ref.dtype)


def _rwkv(p, mu, wl, vecs, bd):
    bsz, seq, _ = p.shape
    cn = RW_STEP_CHUNKS * RW_CHUNK
    const = lambda b, c: (0, 0)
    return pl.pallas_call(
        _rwkv_kernel,
        grid=(bsz, seq // cn),
        in_specs=[
            pl.BlockSpec((1, cn, RW_IN), lambda b, c: (b, c, OFF_RW // RW_IN)),
            pl.BlockSpec((1, RW_IN), const),
            pl.BlockSpec((RW_LORA, 3 * RW_WIDTH), const),
            pl.BlockSpec((SUBLANES, RW_WIDTH), const),
            pl.BlockSpec((RW_UNIT_LANES, RW_UNIT_LANES), const),
        ],
        out_specs=pl.BlockSpec((1, cn, RW_WIDTH), lambda b, c: (b, c, 0)),
        out_shape=jax.ShapeDtypeStruct((bsz, seq, RW_WIDTH), ACT_DTYPE),
        scratch_shapes=[
            pltpu.VMEM((SUBLANES + cn, RW_IN), F32),
            pltpu.VMEM((RW_WIDTH // RW_UNIT_LANES, LANES, RW_UNIT_LANES), F32),
        ],
        compiler_params=pltpu.CompilerParams(
            dimension_semantics=("parallel", "arbitrary"), vmem_limit_bytes=VMEM_LIMIT),
        name="rwkv7",
    )(p, mu, wl, vecs, bd)


def _sb_kernel(q_ref, k_ref, v_ref, o_ref, lb_scr, l1m_scr):
    tb = SB_BLOCK
    n_pairs = SB_LANES // LANES
    qi = pl.program_id(2)
    q = q_ref[0] * (HEAD_DIM ** -0.5)
    m0 = lax.broadcasted_iota(jnp.int32, (tb, LANES), 1) < HEAD_DIM
    q_stacks = [_pair_stack(q[:, pr * LANES:(pr + 1) * LANES], m0).astype(BF16)
                for pr in range(n_pairs)]
    row = lax.broadcasted_iota(jnp.int32, (2 * tb, tb), 0)
    col = lax.broadcasted_iota(jnp.int32, (2 * tb, tb), 1)
    causal = col < jnp.where(row >= tb, row - tb, row)
    krow = lax.broadcasted_iota(jnp.int32, (tb, tb), 0)
    kcol = lax.broadcasted_iota(jnp.int32, (tb, tb), 1)
    later_keys = jnp.where(krow > kcol, 1.0, 0.0).astype(BF16)

    def scores(j, slot, diag):
        start = pl.multiple_of(j * tb, tb)
        sums = []
        for pr in range(n_pairs):
            k_blk = k_ref[0, pl.ds(start, tb), pr * LANES:(pr + 1) * LANES].astype(BF16)
            z = _dot_nt(q_stacks[pr], k_blk)
            log_beta = jnp.minimum(z, 0.0) - jnp.log(1.0 + jnp.exp(-jnp.abs(z)))
            log_1m = log_beta - z
            if diag:
                log_1m = jnp.where(causal, log_1m, 0.0)
                log_beta = jnp.where(causal, log_beta, SB_MASKED)
            lb_scr[slot, pr] = log_beta
            l1m_scr[slot, pr] = log_1m.astype(BF16)
            sums.append(jnp.sum(log_1m, axis=-1, keepdims=True))
        return tuple(sums)

    def weights(j, slot, run, acc):
        start = pl.multiple_of(j * tb, tb)
        out = []
        for pr in range(n_pairs):
            v_blk = v_ref[0, pl.ds(start, tb), pr * LANES:(pr + 1) * LANES].astype(BF16)
            later = _dot(l1m_scr[slot, pr], later_keys) + run[pr]
            att = jnp.exp(lb_scr[slot, pr] + later)
            out.append(acc[pr] + _dot(att.astype(BF16), v_blk))
        return tuple(out)

    def step(i, carry):
        acc, run, sums = carry
        slot = i & 1
        new_sums = scores(qi - 1 - i, 1 - slot, False)
        acc = weights(qi - i, slot, run, acc)
        run = tuple(r + s for r, s in zip(run, sums))
        return acc, run, new_sums

    acc0 = tuple(jnp.zeros((2 * tb, LANES), F32) for _ in range(n_pairs))
    run0 = tuple(jnp.zeros((2 * tb, 1), F32) for _ in range(n_pairs))
    acc, run, _ = lax.fori_loop(0, qi, step, (acc0, run0, scores(qi, 0, True)))
    acc = weights(0, qi & 1, run, acc)
    o_ref[0] = jnp.concatenate(
        [jnp.where(m0, a[0:tb], a[tb:2 * tb]) for a in acc], axis=1).astype(o_ref.dtype)


def _sb_attention(p):
    bsz, seq, _ = p.shape
    tb = SB_BLOCK
    assert SB_LANES == SB_WIDTH
    window = lambda rows: (pl.Element(1), pl.Element(rows), pl.Element(SB_LANES))
    return pl.pallas_call(
        _sb_kernel,
        grid=(bsz, 1, seq // tb),
        in_specs=[
            pl.BlockSpec(window(tb), lambda b, h, i: (b, i * tb, OFF_SB)),
            pl.BlockSpec(window(seq), lambda b, h, i: (b, 0, OFF_SB + SB_WIDTH)),
            pl.BlockSpec(window(seq), lambda b, h, i: (b, 0, OFF_SB + 2 * SB_WIDTH)),
        ],
        out_specs=pl.BlockSpec((1, tb, SB_LANES), lambda b, h, i: (b, i, 0)),
        out_shape=jax.ShapeDtypeStruct((bsz, seq, SB_WIDTH), ACT_DTYPE),
        scratch_shapes=[
            pltpu.VMEM((2, SB_LANES // LANES, 2 * tb, tb), F32),
            pltpu.VMEM((2, SB_LANES // LANES, 2 * tb, tb), BF16),
        ],
        compiler_params=pltpu.CompilerParams(
            dimension_semantics=("parallel", "parallel", "arbitrary"),
            vmem_limit_bytes=VMEM_LIMIT),
        name="stickbreak",
    )(p, p, p)


def _mamba_kernel(xbc_ref, z_ref, dt_ref, cw_ref, cb_ref, hv_ref, dn_ref, ex_ref, o_ref,
                  xpad, state):
    ln = M2_CHUNK
    c = pl.program_id(1)

    halo = M2_CONV_HALO

    @pl.when(c == 0)
    def _():
        xpad[0:halo, :] = jnp.zeros((halo, M2_CONV_DIM), xpad.dtype)
        state[...] = jnp.zeros_like(state)

    x = xbc_ref[0]
    xpad[halo:halo + ln, :] = x
    xp = xpad[...]
    xpad[0:halo, :] = x[ln - halo:ln, :]
    trow = lax.broadcasted_iota(jnp.int32, (ln, halo + ln), 0)
    tcol = lax.broadcasted_iota(jnp.int32, (ln, halo + ln), 1)
    conv = cb_ref[...] + cw_ref[M2_CONV - 1:M2_CONV, :] * x.astype(F32)
    for i in range(M2_CONV - 1):
        sh = M2_CONV - 1 - i
        pick = jnp.where(tcol == trow + (halo - sh), 1.0, 0.0).astype(xp.dtype)
        conv = conv + cw_ref[i:i + 1, :] * _dot(pick, xp)
    xa = _silu(conv)
    xs = xa[:, 0:M2_WIDTH]

    dt = _softplus(dt_ref[0].astype(F32) + hv_ref[0:1, :])
    log_a = dt * (-jnp.exp(hv_ref[1:2, :]))
    row = lax.broadcasted_iota(jnp.int32, (ln, ln), 0)
    col = lax.broadcasted_iota(jnp.int32, (ln, ln), 1)
    lower = row >= col
    ltri = jnp.where(lower, 1.0, 0.0).astype(BF16)
    acum = _dot_exact_lhs(ltri, log_a)
    acum_t = acum.T
    ex = ex_ref[...]
    dt_x = _dot_exact_rhs(dt, ex)
    acum_x = _dot_exact_rhs(acum, ex)
    xdt = xs * dt_x

    m0 = lax.broadcasted_iota(jnp.int32, (ln, LANES), 1) < HEAD_DIM
    gw = M2_GROUP_WIDTH
    heads_per_group = M2_HEADS // M2_GROUPS
    y_groups = []
    for gi in range(M2_GROUPS):
        b_g = xa[:, M2_WIDTH + gi * M2_STATE:M2_WIDTH + (gi + 1) * M2_STATE].astype(BF16)
        c_off = M2_WIDTH + M2_GROUPS * M2_STATE
        c_g = xa[:, c_off + gi * M2_STATE:c_off + (gi + 1) * M2_STATE].astype(BF16)
        cb = _dot_nt(c_g, b_g)
        gs = slice(gi * gw, (gi + 1) * gw)
        ac_g = acum_x[:, gs]
        st = state[gi]
        y_off = _dot(c_g, st.astype(BF16)) * jnp.exp(ac_g)
        pairs = []
        for q in range(heads_per_group // 2):
            mats = []
            for hh in range(2):
                h = gi * heads_per_group + 2 * q + hh
                seg = acum[:, h:h + 1] - acum_t[h:h + 1, :]
                mats.append(cb * jnp.exp(jnp.where(lower, seg, -1e30)))
            xp = xdt[:, gi * gw + q * LANES:gi * gw + (q + 1) * LANES]
            pairs.append(_dot(jnp.concatenate(mats, axis=1).astype(BF16),
                              _pair_stack(xp, m0).astype(BF16)))
        y_diag = jnp.concatenate(pairs, axis=1)
        ac_end = ac_g[ln - 1:ln, :]
        to_end = jnp.exp(ac_end - ac_g)
        state[gi] = st * jnp.exp(ac_end) + _dot_tn(b_g, (xdt[:, gs] * to_end).astype(BF16))
        y = y_diag + y_off + dn_ref[0:1, gs] * xs[:, gs]
        y = y * _silu(z_ref[0, :, gs].astype(F32))
        y = y * lax.rsqrt(jnp.mean(y * y, axis=-1, keepdims=True) + NORM_EPS)
        y_groups.append(y * dn_ref[1:2, gs])
    o_ref[0] = jnp.concatenate(y_groups, axis=1).astype(o_ref.dtype)


def _mamba(p, conv_w, conv_b, head_vecs, dn_vecs, expand):
    bsz, seq, _ = p.shape
    ln = M2_CHUNK
    const = lambda b, c: (0, 0)
    return pl.pallas_call(
        _mamba_kernel,
        grid=(bsz, seq // ln),
        in_specs=[
            pl.BlockSpec((pl.Element(1), pl.Element(ln), pl.Element(M2_CONV_DIM)),
                         lambda b, c: (b, c * ln, OFF_XBC)),
            pl.BlockSpec((pl.Element(1), pl.Element(ln), pl.Element(M2_WIDTH)),
                         lambda b, c: (b, c * ln, OFF_Z)),
            pl.BlockSpec((1, ln, LANES), lambda b, c: (b, c, OFF_DT // LANES)),
            pl.BlockSpec((M2_CONV, M2_CONV_DIM), const),
            pl.BlockSpec((1, M2_CONV_DIM), const),
            pl.BlockSpec((SUBLANES, LANES), const),
            pl.BlockSpec((SUBLANES, M2_WIDTH), const),
            pl.BlockSpec((LANES, M2_WIDTH), const),
        ],
        out_specs=pl.BlockSpec((1, ln, M2_WIDTH), lambda b, c: (b, c, 0)),
        out_shape=jax.ShapeDtypeStruct((bsz, seq, M2_WIDTH), ACT_DTYPE),
        scratch_shapes=[
            pltpu.VMEM((M2_CONV_HALO + ln, M2_CONV_DIM), ACT_DTYPE),
            pltpu.VMEM((M2_GROUPS, M2_STATE, M2_GROUP_WIDTH), F32),
        ],
        compiler_params=pltpu.CompilerParams(
            dimension_semantics=("parallel", "arbitrary"), vmem_limit_bytes=VMEM_LIMIT),
        name="mamba2",
    )(p, p, p, conv_w, conv_b, head_vecs, dn_vecs, expand)


def _merge_kernel(yrw_ref, ysb_ref, ym2_ref, pg_ref, x_ref, mod_ref, g2_ref,
                  wrw_ref, wsb_ref, wm2_ref, wo_ref, xo_ref, h_ref):
    d = D_MODEL
    pg = pg_ref[0].astype(F32)
    merged = _sigmoid(pg[:, 0:d]) * _dot(yrw_ref[0].astype(BF16), wrw_ref[...])
    merged = merged + _sigmoid(pg[:, d:2 * d]) * _dot(ysb_ref[0].astype(BF16), wsb_ref[...])
    merged = merged + _sigmoid(pg[:, 2 * d:3 * d]) * _dot(ym2_ref[0].astype(BF16), wm2_ref[...])
    xn = x_ref[0] + mod_ref[0, 2:3, :] * _dot(merged.astype(BF16), wo_ref[...])
    xo_ref[0] = xn
    h_ref[0] = _rms_mod(xn, g2_ref[...], mod_ref[0, 4:5, :], mod_ref[0, 3:4, :]).astype(BF16)


def _merge(y_rw, y_sb, y_m2, p, x, mod, g2, w_rw, w_sb, w_m2, w_o, layer, tm):
    bsz, seq, d = x.shape
    const = lambda b, i: (0, 0)
    tok = lambda w: pl.BlockSpec((1, tm, w), lambda b, i: (b, i, 0))
    weight = lambda k: pl.BlockSpec((None, k, d), lambda b, i: (layer, 0, 0))
    return pl.pallas_call(
        _merge_kernel,
        grid=(bsz, seq // tm),
        in_specs=[
            tok(RW_WIDTH), tok(SB_WIDTH), tok(M2_WIDTH),
            tok(3 * d),
            tok(d),
            pl.BlockSpec((1, N_MOD, d), lambda b, i: (b, 0, 0)),
            pl.BlockSpec((1, d), const),
            weight(RW_WIDTH), weight(SB_WIDTH), weight(M2_WIDTH), weight(d),
        ],
        out_specs=[tok(d), tok(d)],
        out_shape=[jax.ShapeDtypeStruct((bsz, seq, d), F32),
                   jax.ShapeDtypeStruct((bsz, seq, d), BF16)],
        compiler_params=pltpu.CompilerParams(
            dimension_semantics=("parallel", "parallel"), vmem_limit_bytes=VMEM_LIMIT),
        name="merge",
    )(y_rw, y_sb, y_m2, p, x, mod, g2.reshape(1, d), w_rw, w_sb, w_m2, w_o)


def _ffn_kernel(h_ref, x_ref, mod_ref, wu_ref, cw_ref, cb_ref, wd_ref, fg_ref, o_ref,
                ubuf, carry, *, final):
    tm = h_ref.shape[1]

    @pl.when(pl.program_id(1) == 0)
    def _():
        carry[...] = jnp.zeros_like(carry)

    h = h_ref[0]
    acc = jnp.zeros((tm, D_MODEL), F32)
    start = 0
    for fc in FFN_COL_CHUNKS:
        halves = []
        for half in range(2):
            cols = slice(half * D_FF + start, half * D_FF + start + fc)
            u = _dot(h, wu_ref[:, cols])
            ubuf[0:SUBLANES, 0:fc] = carry[:, cols]
            ubuf[SUBLANES:SUBLANES + tm, 0:fc] = u
            carry[:, cols] = u[tm - SUBLANES:tm, :]
            conv = cb_ref[:, cols] + cw_ref[FFN_CONV - 1:FFN_CONV, cols] * u
            for i in range(FFN_CONV - 1):
                sh = FFN_CONV - 1 - i
                conv = conv + (cw_ref[i:i + 1, cols]
                               * ubuf[SUBLANES - sh:SUBLANES - sh + tm, 0:fc])
            halves.append(conv)
        act = _silu(halves[0]) * halves[1]
        acc = acc + _dot(act.astype(BF16), wd_ref[start:start + fc, :])
        start += fc
    xn = x_ref[0] + mod_ref[0, 5:6, :] * acc
    if final:
        xn = xn * lax.rsqrt(jnp.mean(xn * xn, axis=-1, keepdims=True) + NORM_EPS) * fg_ref[...]
    o_ref[0] = xn


def _ffn(h2, x, mod, w_up, conv_w, conv_b, w_down, final_g, layer, tm, final):
    bsz, seq, d = x.shape
    const = lambda b, i: (0, 0)
    tok = pl.BlockSpec((1, tm, d), lambda b, i: (b, i, 0))
    resident = lambda shape: pl.BlockSpec((None,) + shape, lambda b, i: (layer, 0, 0),
                                          pipeline_mode=pl.Buffered(1))
    return pl.pallas_call(
        functools.partial(_ffn_kernel, final=final),
        grid=(bsz, seq // tm),
        in_specs=[
            tok, tok,
            pl.BlockSpec((1, N_MOD, d), lambda b, i: (b, 0, 0)),
            resident((d, 2 * D_FF)),
            pl.BlockSpec((FFN_CONV, 2 * D_FF), const),
            pl.BlockSpec((1, 2 * D_FF), const),
            resident((D_FF, d)),
            pl.BlockSpec((1, d), const),
        ],
        out_specs=tok,
        out_shape=jax.ShapeDtypeStruct((bsz, seq, d), F32),
        scratch_shapes=[
            pltpu.VMEM((SUBLANES + tm, max(FFN_COL_CHUNKS)), F32),
            pltpu.VMEM((SUBLANES, 2 * D_FF), F32),
        ],
        compiler_params=pltpu.CompilerParams(
            dimension_semantics=("parallel", "arbitrary"), vmem_limit_bytes=VMEM_LIMIT),
        name="convffn",
    )(h2, x, mod, w_up, conv_w, conv_b.reshape(1, -1), w_down, final_g.reshape(1, d))


def _in_weights(w_in):
    pad = IN_MAIN_TILES * IN_COL_TILE - N_MAIN
    w_main = jnp.pad(w_in[..., :N_MAIN].astype(BF16), ((0, 0),) * (w_in.ndim - 1) + ((0, pad),))
    return w_main, w_in[..., N_MAIN:].astype(BF16)


def _rwkv_lora_weight(w2, a2, g2):
    n = RW_WIDTH
    return jnp.concatenate([jnp.pad(w2, ((0, 0), (0, 2 * n))),
                            jnp.pad(a2, ((0, 0), (n, n))),
                            jnp.pad(g2, ((0, 0), (2 * n, 0)))], axis=0).astype(BF16)


def _rows(vectors, width, rows=SUBLANES):
    padded = [jnp.pad(vec.astype(F32), (0, width - vec.shape[0])) for vec in vectors]
    padded += [jnp.zeros((width,), F32)] * (rows - len(vectors))
    return jnp.stack(padded)


def kernel(x, c, ada_w, ada_b, norm1_g, norm2_g, w_in, rw_mu, rw_w0, rw_w2, rw_a0, rw_a2,
           rw_g2, rw_k_k, rw_k_a, rw_r_k, rw_ln_g, rw_ln_b, rw_wo, sb_wo, m2_conv_w,
           m2_conv_b, m2_dt_bias, m2_a_log, m2_d, m2_norm_g, m2_wo, w_out, ffn_w_up,
           ffn_conv_w, ffn_conv_b, ffn_w_down, final_norm_g):
    bsz, seq, d = x.shape
    depth = ada_w.shape[0]
    assert d == D_MODEL and seq % SB_BLOCK == 0
    tm_in = min(2048, seq)
    tm_merge = min(512, seq)
    tm_ffn = min(512, seq)

    head = jnp.arange(RW_UNIT_LANES) // HEAD_DIM
    same_head = (head[:, None] == head[None, :]).astype(BF16)
    lane_head = jnp.arange(M2_WIDTH) // HEAD_DIM
    expand = (jnp.arange(LANES)[:, None] == lane_head[None, :]).astype(BF16)

    w_main, w_gate = _in_weights(w_in)
    w_rw, w_sb, w_m2, w_o, w_up, w_down = (
        w.astype(BF16) for w in (rw_wo, sb_wo, m2_wo, w_out, ffn_w_up, ffn_w_down))

    mod_all = _modulation(c, ada_w, ada_b)
    for l in range(depth):
        mod = mod_all[l].reshape(bsz, N_MOD, d)
        p, p_gate = _inproj(x, mod, norm1_g[l], w_main, w_gate, l, tm_in)
        rw_vecs = _rows([rw_w0[l], rw_a0[l], rw_k_k[l], rw_k_a[l], rw_r_k[l].reshape(-1),
                         rw_ln_g[l], rw_ln_b[l]], RW_WIDTH)
        y_rw = _rwkv(p, rw_mu[l].reshape(1, RW_IN),
                     _rwkv_lora_weight(rw_w2[l], rw_a2[l], rw_g2[l]), rw_vecs, same_head)
        y_sb = _sb_attention(p)
        head_vecs = _rows([m2_dt_bias[l], m2_a_log[l]], LANES)
        dn_vecs = _rows([jnp.repeat(m2_d[l], HEAD_DIM), m2_norm_g[l]], M2_WIDTH)
        y_m2 = _mamba(p, m2_conv_w[l], m2_conv_b[l].reshape(1, -1), head_vecs, dn_vecs, expand)
        x, h2 = _merge(y_rw, y_sb, y_m2, p_gate, x, mod, norm2_g[l], w_rw, w_sb, w_m2, w_o, l,
                       tm_merge)
        x = _ffn(h2, x, mod, w_up, ffn_conv_w[l], ffn_conv_b[l], w_down, final_norm_g, l,
                 tm_ffn, final=(l == depth - 1))
    return x
```
